```python
import math
import jax, jax.numpy as jnp
from jax import lax
import numpy as np

D_MODEL = 2048
BATCH = 2
SEQ = 16384
DEPTH = 2

MEM_LEN = 256
MEM_HEADS = 4
MEM_HEAD_DIM = 128
MEM_WIDTH = MEM_HEADS * MEM_HEAD_DIM
MIX_WIDTH = D_MODEL - MEM_WIDTH
POOL_WINDOWS = (2, 4, 8, 16)
N_POOL_GROUPS = len(POOL_WINDOWS)
POOL_GROUP_DIM = MIX_WIDTH // N_POOL_GROUPS
DIFF_HEADS = 12
DIFF_QK_DIM = 64
DIFF_V_DIM = 2 * DIFF_QK_DIM
ROPE_DIM = DIFF_QK_DIM // 4
ROPE_THETA = 500000.0
Q_BLOCK = 128
D_FF = 5632
CONV_WIDTH = 3
LN_EPS = 1e-5
RMS_EPS = 1e-6
DEEPNORM_ALPHA = (2.0 * DEPTH) ** 0.25
DEEPNORM_BETA = (8.0 * DEPTH) ** -0.25
N_POOL_LAYERS = (DEPTH + 1) // 2
N_DIFF_LAYERS = DEPTH // 2
POOL_IN_WIDTH = MIX_WIDTH + MEM_WIDTH
DIFF_IN_WIDTH = 3 * MIX_WIDTH + MEM_WIDTH

kernel_name = 'hybrid_pool_diffattn_deepnorm'


def _layernorm(x, g, b):
    xf = x.astype(jnp.float32)
    mu = jnp.mean(xf, axis=-1, keepdims=True)
    var = jnp.mean(jnp.square(xf - mu), axis=-1, keepdims=True)
    y = (xf - mu) * lax.rsqrt(var + LN_EPS)
    return (y * g.astype(jnp.float32) + b.astype(jnp.float32)).astype(x.dtype)


def _rmsnorm(x, g):
    xf = x.astype(jnp.float32)
    y = xf * lax.rsqrt(jnp.mean(jnp.square(xf), axis=-1, keepdims=True) + RMS_EPS)
    return (y * g.astype(jnp.float32)).astype(x.dtype)


def _rotary(x, cos, sin):
    half = ROPE_DIM // 2
    x1 = x[..., :half]
    x2 = x[..., half:ROPE_DIM]
    return jnp.concatenate([x1 * cos - x2 * sin, x2 * cos + x1 * sin, x[..., ROPE_DIM:]], axis=-1)


def _pool_mixer(h, w_groups, scale):
    B, S, _ = h.shape
    hf = h.astype(jnp.float32).reshape(B, S, N_POOL_GROUPS, POOL_GROUP_DIM)
    csum = jnp.cumsum(hf, axis=1)
    t = jnp.arange(1, S + 1, dtype=jnp.float32)
    outs = []
    for g, w in enumerate(POOL_WINDOWS):
        cg = csum[:, :, g]
        lag = jnp.pad(cg, ((0, 0), (w, 0), (0, 0)))[:, :S]
        cnt = jnp.minimum(t, float(w))[None, :, None]
        outs.append((cg - lag) / cnt - hf[:, :, g])
    d = jnp.stack(outs, axis=2).astype(h.dtype)
    y = jnp.einsum('bsgc,gcd->bsgd', d, w_groups).reshape(B, S, MIX_WIDTH)
    return y * scale


def _diff_attention(q, k, v, lam):
    B, S = q.shape[:2]
    n_blk = S // Q_BLOCK
    scale = DIFF_QK_DIM ** -0.5
    qb = jnp.moveaxis(q.reshape(B, n_blk, Q_BLOCK, DIFF_HEADS, 2, DIFF_QK_DIM), 1, 0)
    starts = jnp.arange(n_blk, dtype=jnp.int32) * Q_BLOCK
    key_pos = jnp.arange(S, dtype=jnp.int32)

    def one_block(args):
        q_blk, start = args
        s = jnp.einsum('bqhcd,bkhcd->bhcqk', q_blk, k).astype(jnp.float32) * scale
        q_pos = start + jnp.arange(Q_BLOCK, dtype=jnp.int32)
        causal = key_pos[None, :] <= q_pos[:, None]
        s = jnp.where(causal, s, -jnp.inf)
        p = jax.nn.softmax(s, axis=-1)
        a = p[:, :, 0] - lam * p[:, :, 1]
        return jnp.einsum('bhqk,bkhd->bqhd', a.astype(v.dtype), v)

    o = lax.map(one_block, (qb, starts))
    return jnp.moveaxis(o, 0, 1).reshape(B, S, DIFF_HEADS, DIFF_V_DIM)


def _mem_attention(q_mem, mem, w_kv):
    B, S, _ = q_mem.shape
    q = q_mem.reshape(B, S, MEM_HEADS, MEM_HEAD_DIM)
    kv = mem @ w_kv
    k = kv[..., :MEM_WIDTH].reshape(B, MEM_LEN, MEM_HEADS, MEM_HEAD_DIM)
    v = kv[..., MEM_WIDTH:].reshape(B, MEM_LEN, MEM_HEADS, MEM_HEAD_DIM)
    s = jnp.einsum('bshd,bmhd->bhsm', q, k).astype(jnp.float32) * (MEM_HEAD_DIM ** -0.5)
    p = jax.nn.softmax(s, axis=-1)
    o = jnp.einsum('bhsm,bmhd->bshd', p.astype(v.dtype), v)
    return o.reshape(B, S, MEM_WIDTH)


def _conv_ffn(x, w_up, conv_w, conv_b, w_down):
    S = x.shape[1]
    h = x @ w_up
    hp = jnp.pad(h, ((0, 0), (CONV_WIDTH - 1, 0), (0, 0)))
    hc = conv_b + sum(conv_w[kk] * hp[:, kk:kk + S] for kk in range(CONV_WIDTH))
    gate = hc[..., :D_FF]
    up = hc[..., D_FF:]
    return (jax.nn.silu(gate) * up) @ w_down


def setup_inputs(seed: int = 0) -> dict:
    key = jax.random.key(seed)
    ks = jax.random.split(key, 24)

    def nrm(k, shape, scale):
        return jax.random.normal(k, shape, jnp.float32) * scale

    x = nrm(ks[0], (BATCH, SEQ, D_MODEL), 1.0)
    mem = nrm(ks[1], (BATCH, MEM_LEN, D_MODEL), 1.0)
    positions = jnp.broadcast_to(jnp.arange(SEQ, dtype=jnp.int32)[None, :], (BATCH, SEQ))
    pool_w_in = nrm(ks[2], (N_POOL_LAYERS, D_MODEL, POOL_IN_WIDTH), D_MODEL ** -0.5)
    pool_w = nrm(ks[3], (N_POOL_LAYERS, N_POOL_GROUPS, POOL_GROUP_DIM, POOL_GROUP_DIM), POOL_GROUP_DIM ** -0.5)
    pool_scale = 1.0 + nrm(ks[4], (N_POOL_LAYERS, MIX_WIDTH), 0.1)
    diff_w_in = nrm(ks[5], (N_DIFF_LAYERS, D_MODEL, DIFF_IN_WIDTH), D_MODEL ** -0.5)
    diff_lambda_q1 = nrm(ks[6], (N_DIFF_LAYERS, DIFF_QK_DIM), 0.1)
    diff_lambda_k1 = nrm(ks[7], (N_DIFF_LAYERS, DIFF_QK_DIM), 0.1)
    diff_lambda_q2 = nrm(ks[8], (N_DIFF_LAYERS, DIFF_QK_DIM), 0.1)
    diff_lambda_k2 = nrm(ks[9], (N_DIFF_LAYERS, DIFF_QK_DIM), 0.1)
    diff_subln_g = 1.0 + nrm(ks[10], (N_DIFF_LAYERS, DIFF_V_DIM), 0.02)
    mem_w_kv = nrm(ks[11], (DEPTH, D_MODEL, 2 * MEM_WIDTH), D_MODEL ** -0.5)
    w_out = nrm(ks[12], (DEPTH, D_MODEL, D_MODEL), DEEPNORM_BETA * D_MODEL ** -0.5)
    ln1_g = 1.0 + nrm(ks[13], (DEPTH, D_MODEL), 0.02)
    ln1_b = nrm(ks[14], (DEPTH, D_MODEL), 0.02)
    ffn_w_up = nrm(ks[15], (DEPTH, D_MODEL, 2 * D_FF), D_MODEL ** -0.5)
    ffn_conv_w = nrm(ks[16], (DEPTH, CONV_WIDTH, 2 * D_FF), CONV_WIDTH ** -0.5)
    ffn_conv_b = nrm(ks[17], (DEPTH, 2 * D_FF), 0.02)
    ffn_w_down = nrm(ks[18], (DEPTH, D_FF, D_MODEL), DEEPNORM_BETA * D_FF ** -0.5)
    ln2_g = 1.0 + nrm(ks[19], (DEPTH, D_MODEL), 0.02)
    ln2_b = nrm(ks[20], (DEPTH, D_MODEL), 0.02)
    return {'x': x, 'mem': mem, 'positions': positions,
            'pool_w_in': pool_w_in, 'pool_w': pool_w, 'pool_scale': pool_scale,
            'diff_w_in': diff_w_in, 'diff_lambda_q1': diff_lambda_q1, 'diff_lambda_k1': diff_lambda_k1,
            'diff_lambda_q2': diff_lambda_q2, 'diff_lambda_k2': diff_lambda_k2, 'diff_subln_g': diff_subln_g,
            'mem_w_kv': mem_w_kv, 'w_out': w_out, 'ln1_g': ln1_g, 'ln1_b': ln1_b,
            'ffn_w_up': ffn_w_up, 'ffn_conv_w': ffn_conv_w, 'ffn_conv_b': ffn_conv_b,
            'ffn_w_down': ffn_w_down, 'ln2_g': ln2_g, 'ln2_b': ln2_b}


def reference(x, mem, positions, pool_w_in, pool_w, pool_scale, diff_w_in, diff_lambda_q1,
              diff_lambda_k1, diff_lambda_q2, diff_lambda_k2, diff_subln_g, mem_w_kv, w_out,
              ln1_g, ln1_b, ffn_w_up, ffn_conv_w, ffn_conv_b, ffn_w_down, ln2_g, ln2_b):
    B, S, _ = x.shape
    inv_freq = ROPE_THETA ** (-jnp.arange(0, ROPE_DIM, 2, dtype=jnp.float32) / ROPE_DIM)
    ang = positions.astype(jnp.float32)[..., None] * inv_freq
    cos = jnp.cos(ang)[:, :, None, None, :].astype(x.dtype)
    sin = jnp.sin(ang)[:, :, None, None, :].astype(x.dtype)

    for i in range(DEPTH):
        j = i // 2
        if i % 2 == 0:
            h = x @ pool_w_in[j]
            mix = _pool_mixer(h[..., :MIX_WIDTH], pool_w[j], pool_scale[j])
            q_mem = h[..., MIX_WIDTH:]
        else:
            h = x @ diff_w_in[j]
            q = h[..., :MIX_WIDTH].reshape(B, S, DIFF_HEADS, 2, DIFF_QK_DIM)
            k = h[..., MIX_WIDTH:2 * MIX_WIDTH].reshape(B, S, DIFF_HEADS, 2, DIFF_QK_DIM)
            v = h[..., 2 * MIX_WIDTH:3 * MIX_WIDTH].reshape(B, S, DIFF_HEADS, DIFF_V_DIM)
            q_mem = h[..., 3 * MIX_WIDTH:]
            q = _rotary(q, cos, sin)
            k = _rotary(k, cos, sin)
            lam_init = 0.8 - 0.6 * math.exp(-0.3 * i)
            lam = (jnp.exp(jnp.sum(diff_lambda_q1[j].astype(jnp.float32) * diff_lambda_k1[j].astype(jnp.float32)))
                   - jnp.exp(jnp.sum(diff_lambda_q2[j].astype(jnp.float32) * diff_lambda_k2[j].astype(jnp.float32)))
                   + lam_init)
            o = _diff_attention(q, k, v, lam)
            mix = (_rmsnorm(o, diff_subln_g[j]) * (1.0 - lam_init)).reshape(B, S, MIX_WIDTH)
        mem_o = _mem_attention(q_mem, mem, mem_w_kv[i])
        y = jnp.concatenate([mix, mem_o], axis=-1) @ w_out[i]
        x = _layernorm(DEEPNORM_ALPHA * x + y, ln1_g[i], ln1_b[i])
        f = _conv_ffn(x, ffn_w_up[i], ffn_conv_w[i], ffn_conv_b[i], ffn_w_down[i])
        x = _layernorm(DEEPNORM_ALPHA * x + f, ln2_g[i], ln2_b[i])
    return x
```

```python
import functools
import math

import jax
import jax.numpy as jnp
from jax import lax
from jax.experimental import pallas as pl
from jax.experimental.pallas import tpu as pltpu

D_MODEL = 2048
DEPTH = 2
MEM_LEN = 256
MEM_HEADS = 4
MEM_HEAD_DIM = 128
MEM_WIDTH = MEM_HEADS * MEM_HEAD_DIM
MIX_WIDTH = D_MODEL - MEM_WIDTH
POOL_WINDOWS = (2, 4, 8, 16)
POOL_GROUP_DIM = MIX_WIDTH // len(POOL_WINDOWS)
POOL_HALO = 16
DIFF_HEADS = 12
DIFF_QK_DIM = 64
DIFF_V_DIM = 2 * DIFF_QK_DIM
ROPE_DIM = DIFF_QK_DIM // 4
ROPE_HALF = ROPE_DIM // 2
ROPE_THETA = 500000.0
D_FF = 5632
CONV_WIDTH = 3
CONV_HALO = 8
LN_EPS = 1e-5
RMS_EPS = 1e-6
DEEPNORM_ALPHA = (2.0 * DEPTH) ** 0.25
DIFF_IN_WIDTH = 3 * MIX_WIDTH + MEM_WIDTH
MASK_VALUE = -1e30

LANES = 128
VMEM_LIMIT = 56 * 1024 * 1024

BF16 = jnp.bfloat16
F32 = jnp.float32


def _params(n_axes):
    return pltpu.CompilerParams(dimension_semantics=("arbitrary",) * n_axes,
                                vmem_limit_bytes=VMEM_LIMIT)


def _const_spec(shape):
    return pl.BlockSpec(shape, lambda *_: (0,) * len(shape), pipeline_mode=pl.Buffered(1))


def _layernorm(z, g, b):
    mu = jnp.mean(z, axis=-1, keepdims=True)
    zc = z - mu
    var = jnp.mean(zc * zc, axis=-1, keepdims=True)
    return zc * lax.rsqrt(var + LN_EPS) * g + b


def _matmul_kernel(x_ref, w_ref, o_ref, xb_ref):
    @pl.when(pl.program_id(1) == 0)
    def _():
        xb_ref[...] = x_ref[...].astype(BF16)

    o_ref[...] = jnp.dot(xb_ref[...], w_ref[...], preferred_element_type=F32).astype(o_ref.dtype)


def _matmul(x, w, out_dtype, tm, tn):
    m, k = x.shape
    n = w.shape[1]
    return pl.pallas_call(
        _matmul_kernel,
        grid=(m // tm, n // tn),
        in_specs=[pl.BlockSpec((tm, k), lambda i, j: (i, 0)),
                  pl.BlockSpec((k, tn), lambda i, j: (0, j))],
        out_specs=pl.BlockSpec((tm, tn), lambda i, j: (i, j)),
        out_shape=jax.ShapeDtypeStruct((m, n), out_dtype),
        scratch_shapes=[pltpu.VMEM((tm, k), BF16)],
        compiler_params=_params(2),
        name="matmul",
    )(x, w)


def _inproj_rope_kernel(x_ref, w_ref, pos_ref, invf_ref, o_ref, xb_ref, tab_ref, *, tn):
    j = pl.program_id(1)
    n_qk_tiles = 2 * MIX_WIDTH // tn
    n_q_tiles = MIX_WIDTH // tn

    @pl.when(j == 0)
    def _():
        xb_ref[...] = x_ref[...].astype(BF16)
        ang = pos_ref[...].astype(F32) * invf_ref[...]
        cos = jnp.cos(ang)
        sin = jnp.sin(ang)
        r = lax.broadcasted_iota(jnp.int32, ang.shape, 1) % DIFF_QK_DIM
        tab_ref[0] = jnp.where(r < ROPE_DIM, cos, 1.0)
        tab_ref[1] = jnp.where(r < ROPE_HALF, -sin, 0.0)
        tab_ref[2] = jnp.where((r >= ROPE_HALF) & (r < ROPE_DIM), sin, 0.0)

    acc = jnp.dot(xb_ref[...], w_ref[...], preferred_element_type=F32)

    @pl.when(j < n_qk_tiles)
    def _():
        sc = jnp.where(j < n_q_tiles, DIFF_QK_DIM ** -0.5, 1.0).astype(F32)
        for c in range(tn // LANES):
            a = acc[:, c * LANES:(c + 1) * LANES]
            up = pltpu.roll(a, LANES - ROPE_HALF, 1)
            dn = pltpu.roll(a, ROPE_HALF, 1)
            y = a * tab_ref[0] + up * tab_ref[1] + dn * tab_ref[2]
            o_ref[:, c * LANES:(c + 1) * LANES] = (y * sc).astype(o_ref.dtype)

    @pl.when(j >= n_qk_tiles)
    def _():
        o_ref[...] = acc.astype(o_ref.dtype)


def _inproj_rope(x, w, pos, invf, tm, tn):
    m, k = x.shape
    n = w.shape[1]
    return pl.pallas_call(
        functools.partial(_inproj_rope_kernel, tn=tn),
        grid=(m // tm, n // tn),
        in_specs=[pl.BlockSpec((tm, k), lambda i, j: (i, 0)),
                  pl.BlockSpec((k, tn), lambda i, j: (0, j)),
                  pl.BlockSpec((tm, 1), lambda i, j: (i, 0)),
                  pl.BlockSpec((1, LANES), lambda i, j: (0, 0))],
        out_specs=pl.BlockSpec((tm, tn), lambda i, j: (i, j)),
        out_shape=jax.ShapeDtypeStruct((m, n), BF16),
        scratch_shapes=[pltpu.VMEM((tm, k), BF16), pltpu.VMEM((3, tm, LANES), F32)],
        compiler_params=_params(2),
        name="inproj_rope",
    )(x, w, pos, invf)


def _pool_kernel(h_ref, w_ref, scale_ref, o_ref, ext_ref, *, tm):
    i = pl.program_id(1)

    @pl.when(i == 0)
    def _():
        ext_ref[0:POOL_HALO, :] = jnp.zeros((POOL_HALO, MIX_WIDTH), F32)

    @pl.when(i > 0)
    def _():
        ext_ref[0:POOL_HALO, :] = ext_ref[tm:tm + POOL_HALO, :]

    ext_ref[POOL_HALO:POOL_HALO + tm, :] = h_ref[...]

    pos = i * tm + lax.broadcasted_iota(jnp.int32, (tm, 1), 0)
    for g, win in enumerate(POOL_WINDOWS):
        cols = slice(g * POOL_GROUP_DIM, (g + 1) * POOL_GROUP_DIM)
        cur = ext_ref[POOL_HALO:POOL_HALO + tm, cols]
        tot = cur
        for lag in range(1, win):
            tot = tot + ext_ref[POOL_HALO - lag:POOL_HALO - lag + tm, cols]
        cnt = jnp.minimum(pos + 1, win).astype(F32)
        d = tot / cnt - cur
        y = jnp.dot(d.astype(BF16), w_ref[g], preferred_element_type=F32)
        o_ref[:, cols] = (y * scale_ref[:, cols]).astype(o_ref.dtype)


def _pool(h, w_groups, scale, batch, seq, tm):
    n_s = seq // tm
    return pl.pallas_call(
        functools.partial(_pool_kernel, tm=tm),
        grid=(batch, n_s),
        in_specs=[pl.BlockSpec((tm, MIX_WIDTH), lambda b, i: (b * n_s + i, 0)),
                  _const_spec(w_groups.shape),
                  _const_spec(scale.shape)],
        out_specs=pl.BlockSpec((tm, MIX_WIDTH), lambda b, i: (b * n_s + i, 0)),
        out_shape=jax.ShapeDtypeStruct((batch * seq, MIX_WIDTH), BF16),
        scratch_shapes=[pltpu.VMEM((tm + POOL_HALO, MIX_WIDTH), F32)],
        compiler_params=_params(2),
        name="pool_mixer",
    )(h, w_groups, scale)


def _diff_attn_kernel(lq1_ref, lk1_ref, lq2_ref, lk2_ref, g_ref, q_ref, k_ref, v_ref, o_ref,
                      q2_ref, m_ref, l_ref, acc_ref, *, tq, lam_init):
    i = pl.program_id(2)
    tk = tq

    q = q_ref[...]
    lane = lax.broadcasted_iota(jnp.int32, q.shape, 1)
    zero = jnp.zeros_like(q)
    q2_ref[0:tq, :] = jnp.where(lane < DIFF_QK_DIM, q, zero)
    q2_ref[tq:2 * tq, :] = jnp.where(lane >= DIFF_QK_DIM, q, zero)
    m_ref[...] = jnp.full(m_ref.shape, MASK_VALUE, F32)
    l_ref[...] = jnp.zeros(l_ref.shape, F32)
    acc_ref[...] = jnp.zeros(acc_ref.shape, F32)

    def step(j, masked):
        start = pl.multiple_of(j * tk, tk)
        k = k_ref[pl.ds(start, tk), :]
        v = v_ref[pl.ds(start, tk), :]
        s = lax.dot_general(q2_ref[...], k, (((1,), (1,)), ((), ())), preferred_element_type=F32)
        if masked:
            qpos = lax.broadcasted_iota(jnp.int32, s.shape, 0) % tq
            kpos = lax.broadcasted_iota(jnp.int32, s.shape, 1)
            s = jnp.where(kpos <= qpos, s, MASK_VALUE)
        m_old = m_ref[...]
        m_new = jnp.maximum(m_old, jnp.max(s, axis=-1, keepdims=True))
        alpha = jnp.exp(m_old - m_new)
        p = jnp.exp(s - m_new)
        l_ref[...] = alpha * l_ref[...] + jnp.sum(p, axis=-1, keepdims=True)
        acc_ref[...] = alpha * acc_ref[...] + jnp.dot(p.astype(BF16), v, preferred_element_type=F32)
        m_ref[...] = m_new

    def body(j, carry):
        step(j, masked=False)
        return carry

    lax.fori_loop(0, i, body, 0)
    step(i, masked=True)

    lam = (jnp.exp(jnp.sum(lq1_ref[...] * lk1_ref[...], axis=-1, keepdims=True))
           - jnp.exp(jnp.sum(lq2_ref[...] * lk2_ref[...], axis=-1, keepdims=True)) + lam_init)
    o = acc_ref[0:tq, :] / l_ref[0:tq, :] - lam * (acc_ref[tq:2 * tq, :] / l_ref[tq:2 * tq, :])
    y = o * lax.rsqrt(jnp.mean(o * o, axis=-1, keepdims=True) + RMS_EPS) * g_ref[...]
    o_ref[...] = (y * (1.0 - lam_init)).astype(o_ref.dtype)


def _diff_attn(hq, lq1, lk1, lq2, lk2, g, batch, seq, tq, lam_init):
    n_q = seq // tq
    vec = lambda n: pl.BlockSpec((1, n), lambda b, h, i: (0, 0))
    k_off = MIX_WIDTH // LANES
    v_off = 2 * MIX_WIDTH // LANES
    return pl.pallas_call(
        functools.partial(_diff_attn_kernel, tq=tq, lam_init=lam_init),
        grid=(batch, DIFF_HEADS, n_q),
        in_specs=[vec(DIFF_QK_DIM), vec(DIFF_QK_DIM), vec(DIFF_QK_DIM), vec(DIFF_QK_DIM), vec(DIFF_V_DIM),
                  pl.BlockSpec((None, tq, LANES), lambda b, h, i: (b, i, h)),
                  pl.BlockSpec((None, seq, LANES), lambda b, h, i: (b, 0, k_off + h)),
                  pl.BlockSpec((None, seq, LANES), lambda b, h, i: (b, 0, v_off + h))],
        out_specs=pl.BlockSpec((None, tq, LANES), lambda b, h, i: (b, i, h)),
        out_shape=jax.ShapeDtypeStruct((batch, seq, MIX_WIDTH), BF16),
        scratch_shapes=[pltpu.VMEM((2 * tq, LANES), BF16),
                        pltpu.VMEM((2 * tq, 1), F32),
                        pltpu.VMEM((2 * tq, 1), F32),
                        pltpu.VMEM((2 * tq, DIFF_V_DIM), F32)],
        compiler_params=_params(3),
        name="diff_attn",
    )(lq1, lk1, lq2, lk2, g, hq, hq, hq)


def _post_kernel(mix_ref, qm_ref, kv_ref, w_ref, x_ref, g_ref, b_ref, o_ref, cat_ref):
    cat_ref[:, 0:MIX_WIDTH] = mix_ref[...]
    q = qm_ref[...].astype(BF16)
    for hd in range(MEM_HEADS):
        lo = hd * MEM_HEAD_DIM
        qh = q[:, lo:lo + MEM_HEAD_DIM]
        kh = kv_ref[:, lo:lo + MEM_HEAD_DIM]
        vh = kv_ref[:, MEM_WIDTH + lo:MEM_WIDTH + lo + MEM_HEAD_DIM]
        s = lax.dot_general(qh, kh, (((1,), (1,)), ((), ())), preferred_element_type=F32)
        s = s * (MEM_HEAD_DIM ** -0.5)
        p = jnp.exp(s - jnp.max(s, axis=-1, keepdims=True))
        denom = jnp.sum(p, axis=-1, keepdims=True)
        oh = jnp.dot(p.astype(BF16), vh, preferred_element_type=F32) / denom
        cat_ref[:, MIX_WIDTH + lo:MIX_WIDTH + lo + MEM_HEAD_DIM] = oh.astype(BF16)
    y = jnp.dot(cat_ref[...], w_ref[...], preferred_element_type=F32)
    o_ref[...] = _layernorm(DEEPNORM_ALPHA * x_ref[...] + y, g_ref[...], b_ref[...])


def _post(mix, qm_src, qm_block, kv, w_out, x, g, b, seq, tm):
    t = x.shape[0]
    n_s = seq // tm
    return pl.pallas_call(
        _post_kernel,
        grid=(t // tm,),
        in_specs=[pl.BlockSpec((tm, MIX_WIDTH), lambda i: (i, 0)),
                  pl.BlockSpec((tm, MEM_WIDTH), lambda i: (i, qm_block)),
                  pl.BlockSpec((MEM_LEN, 2 * MEM_WIDTH), lambda i: (i // n_s, 0)),
                  _const_spec(w_out.shape),
                  pl.BlockSpec((tm, D_MODEL), lambda i: (i, 0)),
                  _const_spec(g.shape),
                  _const_spec(b.shape)],
        out_specs=pl.BlockSpec((tm, D_MODEL), lambda i: (i, 0)),
        out_shape=jax.ShapeDtypeStruct((t, D_MODEL), F32),
        scratch_shapes=[pltpu.VMEM((tm, D_MODEL), BF16)],
        compiler_params=_params(1),
        name="post_mix",
    )(mix, qm_src, kv, w_out, x, g, b)


def _ffn_kernel(x_ref, wg_ref, wu_ref, cwg_ref, cwu_ref, cbg_ref, cbu_ref, wd_ref, g_ref, b_ref, o_ref,
                xb_ref, acc_ref, hg_ref, hu_ref, carry_g_ref, carry_u_ref, *, tm, tiles_per_seq):
    i = pl.program_id(0)
    f = pl.program_id(1)
    n_f = pl.num_programs(1)
    seq_start = (i % tiles_per_seq) == 0

    @pl.when(f == 0)
    def _():
        xb_ref[...] = x_ref[...].astype(BF16)

    def conv_branch(w_ref, cw_ref, cb_ref, h_ref, carry_ref):
        h = jnp.dot(xb_ref[...], w_ref[...], preferred_element_type=F32)
        @pl.when(seq_start)
        def _():
            h_ref[0:CONV_HALO, :] = jnp.zeros((CONV_HALO, h.shape[1]), F32)

        @pl.when(jnp.logical_not(seq_start))
        def _():
            h_ref[0:CONV_HALO, :] = carry_ref[f]

        h_ref[CONV_HALO:CONV_HALO + tm, :] = h
        carry_ref[f] = h[tm - CONV_HALO:tm, :]
        return (cb_ref[...] + cw_ref[2:3, :] * h
                + cw_ref[1:2, :] * h_ref[CONV_HALO - 1:CONV_HALO - 1 + tm, :]
                + cw_ref[0:1, :] * h_ref[CONV_HALO - 2:CONV_HALO - 2 + tm, :])

    gate = conv_branch(wg_ref, cwg_ref, cbg_ref, hg_ref, carry_g_ref)
    up = conv_branch(wu_ref, cwu_ref, cbu_ref, hu_ref, carry_u_ref)
    act = gate * jax.nn.sigmoid(gate) * up
    contrib = jnp.dot(act.astype(BF16), wd_ref[...], preferred_element_type=F32)

    @pl.when(f == 0)
    def _():
        acc_ref[...] = contrib

    @pl.when(f > 0)
    def _():
        acc_ref[...] += contrib

    @pl.when(f == n_f - 1)
    def _():
        o_ref[...] = _layernorm(DEEPNORM_ALPHA * x_ref[...] + acc_ref[...], g_ref[...], b_ref[...])


def _ffn(x, w_up, conv_w, conv_b, w_down, g, b, seq, tm, tf):
    t = x.shape[0]
    n_f = D_FF // tf
    return pl.pallas_call(
        functools.partial(_ffn_kernel, tm=tm, tiles_per_seq=seq // tm),
        grid=(t // tm, n_f),
        in_specs=[pl.BlockSpec((tm, D_MODEL), lambda i, f: (i, 0)),
                  pl.BlockSpec((D_MODEL, tf), lambda i, f: (0, f)),
                  pl.BlockSpec((D_MODEL, tf), lambda i, f: (0, n_f + f)),
                  pl.BlockSpec((CONV_WIDTH, tf), lambda i, f: (0, f)),
                  pl.BlockSpec((CONV_WIDTH, tf), lambda i, f: (0, n_f + f)),
                  pl.BlockSpec((1, tf), lambda i, f: (0, f)),
                  pl.BlockSpec((1, tf), lambda i, f: (0, n_f + f)),
                  pl.BlockSpec((tf, D_MODEL), lambda i, f: (f, 0)),
                  _const_spec(g.shape),
                  _const_spec(b.shape)],
        out_specs=pl.BlockSpec((tm, D_MODEL), lambda i, f: (i, 0)),
        out_shape=jax.ShapeDtypeStruct((t, D_MODEL), F32),
        scratch_shapes=[pltpu.VMEM((tm, D_MODEL), BF16),
                        pltpu.VMEM((tm, D_MODEL), F32),
                        pltpu.VMEM((tm + CONV_HALO, tf), F32),
                        pltpu.VMEM((tm + CONV_HALO, tf), F32),
                        pltpu.VMEM((n_f, CONV_HALO, tf), F32),
                        pltpu.VMEM((n_f, CONV_HALO, tf), F32)],
        compiler_params=_params(2),
        name="conv_ffn",
    )(x, w_up, w_up, conv_w, conv_w, conv_b, conv_b, w_down, g, b)


def kernel(x, mem, positions, pool_w_in, pool_w, pool_scale, diff_w_in, diff_lambda_q1, diff_lambda_k1,
           diff_lambda_q2, diff_lambda_k2, diff_subln_g, mem_w_kv, w_out, ln1_g, ln1_b, ffn_w_up,
           ffn_conv_w, ffn_conv_b, ffn_w_down, ln2_g, ln2_b):
    batch, seq, _ = x.shape
    t = batch * seq
    row = lambda a: a.reshape(1, -1)

    xs = x.reshape(t, D_MODEL)
    mem2 = mem.reshape(batch * MEM_LEN, D_MODEL)
    pos = positions.reshape(t, 1)
    inv_freq = ROPE_THETA ** (-jnp.arange(0, ROPE_DIM, 2, dtype=F32) / ROPE_DIM)
    invf = jnp.tile(jnp.tile(inv_freq, DIFF_QK_DIM // ROPE_HALF), LANES // DIFF_QK_DIM).reshape(1, LANES)

    for i in range(DEPTH):
        j = i // 2
        kv = _matmul(mem2, mem_w_kv[i].astype(BF16), BF16, tm=batch * MEM_LEN, tn=512)
        if i % 2 == 0:
            h = _matmul(xs, pool_w_in[j].astype(BF16), F32, tm=1024, tn=1024)
            mix = _pool(h, pool_w[j].astype(BF16), row(pool_scale[j]), batch, seq, tm=512)
            qm_src, qm_block = h, MIX_WIDTH // MEM_WIDTH
        else:
            hq = _inproj_rope(xs, diff_w_in[j].astype(BF16), pos, invf, tm=1024, tn=512)
            lam_init = 0.8 - 0.6 * math.exp(-0.3 * i)
            mix = _diff_attn(hq.reshape(batch, seq, DIFF_IN_WIDTH), row(diff_lambda_q1[j]),
                             row(diff_lambda_k1[j]), row(diff_lambda_q2[j]), row(diff_lambda_k2[j]),
                             row(diff_subln_g[j]), batch, seq, tq=512, lam_init=lam_init)
            mix = mix.reshape(t, MIX_WIDTH)
            qm_src, qm_block = hq, 3 * MIX_WIDTH // MEM_WIDTH
        xs = _post(mix, qm_src, qm_block, kv, w_out[i].astype(BF16), xs, row(ln1_g[i]), row(ln1_b[i]),
                   seq, tm=512)
        xs = _ffn(xs, ffn_w_up[i].astype(BF16), ffn_conv_w[i], row(ffn_conv_b[i]),
                  ffn_w_down[i].astype(BF16), row(ln2_g[i]), row(ln2_b[i]), seq, tm=512, tf=512)
    return xs.reshape(batch, seq, D_MODEL)
```

```python
import functools
import math

import jax
import jax.numpy as jnp
from jax import lax
from jax.experimental import pallas as pl
from jax.experimental.pallas import tpu as pltpu

D_MODEL = 2048
DEPTH = 2
MEM_LEN = 256
MEM_HEADS = 4
MEM_HEAD_DIM = 128
MEM_WIDTH = MEM_HEADS * MEM_HEAD_DIM
MIX_WIDTH = D_MODEL - MEM_WIDTH
POOL_WINDOWS = (2, 4, 8, 16)
POOL_GROUP_DIM = MIX_WIDTH // len(POOL_WINDOWS)
POOL_HALO = 16
DIFF_HEADS = 12
DIFF_QK_DIM = 64
DIFF_V_DIM = 2 * DIFF_QK_DIM
ROPE_DIM = DIFF_QK_DIM // 4
ROPE_HALF = ROPE_DIM // 2
ROPE_THETA = 500000.0
D_FF = 5632
CONV_WIDTH = 3
CONV_HALO = 8
LN_EPS = 1e-5
RMS_EPS = 1e-6
DEEPNORM_ALPHA = (2.0 * DEPTH) ** 0.25
DIFF_IN_WIDTH = 3 * MIX_WIDTH + MEM_WIDTH
MASK_VALUE = -1e30
QK_SCALE_LOG2E = DIFF_QK_DIM ** -0.5 * math.log2(math.e)

ATTN_TILE = 512
LANES = 128
VMEM_LIMIT = 56 * 1024 * 1024

BF16 = jnp.bfloat16
F32 = jnp.float32


def _params(n_axes):
    return pltpu.CompilerParams(dimension_semantics=("arbitrary",) * n_axes,
                                vmem_limit_bytes=VMEM_LIMIT)


def _const_spec(shape):
    return pl.BlockSpec(shape, lambda *_: (0,) * len(shape), pipeline_mode=pl.Buffered(1))


def _layernorm(z, g, b):
    mu = jnp.mean(z, axis=-1, keepdims=True)
    zc = z - mu
    var = jnp.mean(zc * zc, axis=-1, keepdims=True)
    return zc * lax.rsqrt(var + LN_EPS) * g + b


def _matmul_kernel(x_ref, w_ref, o_ref, xb_ref):
    @pl.when(pl.program_id(1) == 0)
    def _():
        xb_ref[...] = x_ref[...].astype(BF16)

    o_ref[...] = jnp.dot(xb_ref[...], w_ref[...], preferred_element_type=F32).astype(o_ref.dtype)


def _matmul(x, w, out_dtype, tm, tn):
    m, k = x.shape
    n = w.shape[1]
    return pl.pallas_call(
        _matmul_kernel,
        grid=(m // tm, n // tn),
        in_specs=[pl.BlockSpec((tm, k), lambda i, j: (i, 0)),
                  pl.BlockSpec((k, tn), lambda i, j: (0, j))],
        out_specs=pl.BlockSpec((tm, tn), lambda i, j: (i, j)),
        out_shape=jax.ShapeDtypeStruct((m, n), out_dtype),
        scratch_shapes=[pltpu.VMEM((tm, k), BF16)],
        compiler_params=_params(2),
        name="matmul",
    )(x, w)


def _inproj_rope_kernel(x_ref, w_ref, pos_ref, invf_ref, qt_ref, k_ref, vt_ref, qm_ref, xb_ref, tab_ref,
                        *, tm, tn, tk):
    j = pl.program_id(1)
    n_qk_tiles = 2 * MIX_WIDTH // tn
    n_q_tiles = MIX_WIDTH // tn

    @pl.when(j == 0)
    def _():
        xb_ref[...] = x_ref[...].astype(BF16)
        ang = pos_ref[...].astype(F32) * invf_ref[...]
        cos = jnp.cos(ang)
        sin = jnp.sin(ang)
        r = lax.broadcasted_iota(jnp.int32, ang.shape, 1) % DIFF_QK_DIM
        tab_ref[0] = jnp.where(r < ROPE_DIM, cos, 1.0)
        tab_ref[1] = jnp.where(r < ROPE_HALF, -sin, 0.0)
        tab_ref[2] = jnp.where((r >= ROPE_HALF) & (r < ROPE_DIM), sin, 0.0)

    acc = jnp.dot(xb_ref[...], w_ref[...], preferred_element_type=F32)

    def rope(a):
        up = pltpu.roll(a, LANES - ROPE_HALF, 1)
        dn = pltpu.roll(a, ROPE_HALF, 1)
        return a * tab_ref[0] + up * tab_ref[1] + dn * tab_ref[2]

    heads = tn // LANES

    @pl.when(j < n_q_tiles)
    def _():
        for c in range(heads):
            y = rope(acc[:, c * LANES:(c + 1) * LANES]) * QK_SCALE_LOG2E
            qt_ref[c * LANES:(c + 1) * LANES, :] = y.T.astype(BF16)

    @pl.when((j >= n_q_tiles) & (j < n_qk_tiles))
    def _():
        for c in range(heads):
            k_ref[:, c * LANES:(c + 1) * LANES] = rope(acc[:, c * LANES:(c + 1) * LANES]).astype(BF16)

    @pl.when((j >= n_qk_tiles) & (j < n_qk_tiles + n_q_tiles))
    def _():
        for c in range(heads):
            vt = acc[:, c * LANES:(c + 1) * LANES].T
            for s in range(tm // tk):
                vt_ref[c, s] = vt[:, s * tk:(s + 1) * tk].astype(BF16)

    @pl.when(j >= n_qk_tiles + n_q_tiles)
    def _():
        qm_ref[...] = acc.astype(BF16)


def _inproj_rope(x, w, pos, invf, batch, seq, tm, tn, tk):
    m, k = x.shape
    n = w.shape[1]
    assert tn == MEM_WIDTH and MIX_WIDTH % tn == 0
    n_s = seq // tm
    nq = MIX_WIDTH // tn
    clamp = lambda j, lo: jnp.clip(j - lo, 0, nq - 1)
    heads = tn // LANES
    return pl.pallas_call(
        functools.partial(_inproj_rope_kernel, tm=tm, tn=tn, tk=tk),
        grid=(m // tm, n // tn),
        in_specs=[pl.BlockSpec((tm, k), lambda i, j: (i, 0)),
                  pl.BlockSpec((k, tn), lambda i, j: (0, j)),
                  pl.BlockSpec((tm, 1), lambda i, j: (i, 0)),
                  pl.BlockSpec((1, LANES), lambda i, j: (0, 0))],
        out_specs=[pl.BlockSpec((None, tn, tm), lambda i, j: (i // n_s, clamp(j, 0), i % n_s)),
                   pl.BlockSpec((tm, tn), lambda i, j: (i, clamp(j, nq))),
                   pl.BlockSpec((None, heads, tm // tk, LANES, tk),
                                lambda i, j: (i // n_s, clamp(j, 2 * nq), i % n_s, 0, 0)),
                   pl.BlockSpec((tm, tn), lambda i, j: (i, 0))],
        out_shape=[jax.ShapeDtypeStruct((batch, MIX_WIDTH, seq), BF16),
                   jax.ShapeDtypeStruct((m, MIX_WIDTH), BF16),
                   jax.ShapeDtypeStruct((batch, DIFF_HEADS, seq // tk, LANES, tk), BF16),
                   jax.ShapeDtypeStruct((m, MEM_WIDTH), BF16)],
        scratch_shapes=[pltpu.VMEM((tm, k), BF16), pltpu.VMEM((3, tm, LANES), F32)],
        compiler_params=_params(2),
        name="inproj_rope",
    )(x, w, pos, invf)


def _pool_kernel(h_ref, w_ref, scale_ref, o_ref, ext_ref, *, tm):
    i = pl.program_id(1)

    @pl.when(i == 0)
    def _():
        ext_ref[0:POOL_HALO, :] = jnp.zeros((POOL_HALO, MIX_WIDTH), F32)

    @pl.when(i > 0)
    def _():
        ext_ref[0:POOL_HALO, :] = ext_ref[tm:tm + POOL_HALO, :]

    ext_ref[POOL_HALO:POOL_HALO + tm, :] = h_ref[...]

    pos = i * tm + lax.broadcasted_iota(jnp.int32, (tm, 1), 0)
    for g, win in enumerate(POOL_WINDOWS):
        cols = slice(g * POOL_GROUP_DIM, (g + 1) * POOL_GROUP_DIM)
        cur = ext_ref[POOL_HALO:POOL_HALO + tm, cols]
        tot = cur
        for lag in range(1, win):
            tot = tot + ext_ref[POOL_HALO - lag:POOL_HALO - lag + tm, cols]
        cnt = jnp.minimum(pos + 1, win).astype(F32)
        d = tot / cnt - cur
        y = jnp.dot(d.astype(BF16), w_ref[g], preferred_element_type=F32)
        o_ref[:, cols] = (y * scale_ref[:, cols]).astype(o_ref.dtype)


def _pool(h, w_groups, scale, batch, seq, tm):
    n_s = seq // tm
    return pl.pallas_call(
        functools.partial(_pool_kernel, tm=tm),
        grid=(batch, n_s),
        in_specs=[pl.BlockSpec((tm, MIX_WIDTH), lambda b, i: (b * n_s + i, 0)),
                  _const_spec(w_groups.shape),
                  _const_spec(scale.shape)],
        out_specs=pl.BlockSpec((tm, MIX_WIDTH), lambda b, i: (b * n_s + i, 0)),
        out_shape=jax.ShapeDtypeStruct((batch * seq, MIX_WIDTH), BF16),
        scratch_shapes=[pltpu.VMEM((tm + POOL_HALO, MIX_WIDTH), F32)],
        compiler_params=_params(2),
        name="pool_mixer",
    )(h, w_groups, scale)


def _diff_attn_kernel(lq1_ref, lk1_ref, lq2_ref, lk2_ref, g_ref, qt_ref, k_ref, vt_ref, o_ref,
                      w_ref, sa_ref, sb_ref, m_ref, l_ref, acc_ref, *, tq, lam_init):
    i = pl.program_id(2)
    tk = tq

    qt = qt_ref[...]
    row = lax.broadcasted_iota(jnp.int32, qt.shape, 0)
    zero = jnp.zeros_like(qt)
    w_ref[:, 0:tq] = jnp.where(row < DIFF_QK_DIM, qt, zero)
    w_ref[:, tq:2 * tq] = jnp.where(row >= DIFF_QK_DIM, qt, zero)
    m_ref[...] = jnp.full(m_ref.shape, MASK_VALUE, F32)
    l_ref[...] = jnp.zeros(l_ref.shape, F32)
    acc_ref[...] = jnp.zeros(acc_ref.shape, F32)

    def scores(j, s_ref):
        start = pl.multiple_of(j * tk, tk)
        s_ref[...] = jnp.dot(k_ref[pl.ds(start, tk), :], w_ref[...], preferred_element_type=F32)

    def softmax_pv(j, s_ref, masked):
        s = s_ref[...]
        if masked:
            kpos = lax.broadcasted_iota(jnp.int32, s.shape, 0)
            qpos = lax.broadcasted_iota(jnp.int32, s.shape, 1) % tq
            s = jnp.where(kpos <= qpos, s, MASK_VALUE)
        m_old = m_ref[...]
        m_new = jnp.maximum(m_old, jnp.max(s, axis=0, keepdims=True))
        alpha = jnp.exp2(m_old - m_new)
        p = jnp.exp2(s - m_new)
        l_ref[...] = alpha * l_ref[...] + jnp.sum(p, axis=0, keepdims=True)
        acc_ref[...] = alpha * acc_ref[...] + jnp.dot(vt_ref[j], p.astype(BF16), preferred_element_type=F32)
        m_ref[...] = m_new

    scores(0, sa_ref)

    def pair(jj, carry):
        j = 2 * jj
        scores(j + 1, sb_ref)
        softmax_pv(j, sa_ref, masked=False)
        scores(j + 2, sa_ref)
        softmax_pv(j + 1, sb_ref, masked=False)
        return carry

    lax.fori_loop(0, i // 2, pair, 0)

    @pl.when(i % 2 == 0)
    def _():
        softmax_pv(i, sa_ref, masked=True)

    @pl.when(i % 2 == 1)
    def _():
        scores(i, sb_ref)
        softmax_pv(i - 1, sa_ref, masked=False)
        softmax_pv(i, sb_ref, masked=True)

    lam = (jnp.exp(jnp.sum(lq1_ref[...] * lk1_ref[...], axis=-1, keepdims=True))
           - jnp.exp(jnp.sum(lq2_ref[...] * lk2_ref[...], axis=-1, keepdims=True)) + lam_init)
    o1 = acc_ref[:, 0:tq] / l_ref[:, 0:tq]
    o2 = acc_ref[:, tq:2 * tq] / l_ref[:, tq:2 * tq]
    ot = o1 - lam * o2
    ot = ot * lax.rsqrt(jnp.mean(ot * ot, axis=0, keepdims=True) + RMS_EPS)
    o_ref[...] = (ot.T * g_ref[...] * (1.0 - lam_init)).astype(o_ref.dtype)


def _diff_attn(qt, k, vt, lq1, lk1, lq2, lk2, g, batch, seq, tq, lam_init):
    n_q = seq // tq
    vec = lambda n: pl.BlockSpec((1, n), lambda b, h, i: (0, 0))
    return pl.pallas_call(
        functools.partial(_diff_attn_kernel, tq=tq, lam_init=lam_init),
        grid=(batch, DIFF_HEADS, n_q),
        in_specs=[vec(DIFF_QK_DIM), vec(DIFF_QK_DIM), vec(DIFF_QK_DIM), vec(DIFF_QK_DIM), vec(DIFF_V_DIM),
                  pl.BlockSpec((None, LANES, tq), lambda b, h, i: (b, h, i)),
                  pl.BlockSpec((None, seq, LANES), lambda b, h, i: (b, 0, h)),
                  pl.BlockSpec((None, None, n_q, LANES, tq), lambda b, h, i: (b, h, 0, 0, 0))],
        out_specs=pl.BlockSpec((None, tq, LANES), lambda b, h, i: (b, i, h)),
        out_shape=jax.ShapeDtypeStruct((batch, seq, MIX_WIDTH), BF16),
        scratch_shapes=[pltpu.VMEM((LANES, 2 * tq), BF16),
                        pltpu.VMEM((tq, 2 * tq), F32),
                        pltpu.VMEM((tq, 2 * tq), F32),
                        pltpu.VMEM((1, 2 * tq), F32),
                        pltpu.VMEM((1, 2 * tq), F32),
                        pltpu.VMEM((DIFF_V_DIM, 2 * tq), F32)],
        compiler_params=_params(3),
        name="diff_attn",
    )(lq1, lk1, lq2, lk2, g, qt, k, vt)


def _post_kernel(mix_ref, qm_ref, kv_ref, w_ref, x_ref, g_ref, b_ref, o_ref, cat_ref):
    cat_ref[:, 0:MIX_WIDTH] = mix_ref[...]
    q = qm_ref[...].astype(BF16)
    for hd in range(MEM_HEADS):
        lo = hd * MEM_HEAD_DIM
        qh = q[:, lo:lo + MEM_HEAD_DIM]
        kh = kv_ref[:, lo:lo + MEM_HEAD_DIM]
        vh = kv_ref[:, MEM_WIDTH + lo:MEM_WIDTH + lo + MEM_HEAD_DIM]
        s = lax.dot_general(qh, kh, (((1,), (1,)), ((), ())), preferred_element_type=F32)
        s = s * (MEM_HEAD_DIM ** -0.5)
        p = jnp.exp(s - jnp.max(s, axis=-1, keepdims=True))
        denom = jnp.sum(p, axis=-1, keepdims=True)
        oh = jnp.dot(p.astype(BF16), vh, preferred_element_type=F32) / denom
        cat_ref[:, MIX_WIDTH + lo:MIX_WIDTH + lo + MEM_HEAD_DIM] = oh.astype(BF16)
    y = jnp.dot(cat_ref[...], w_ref[...], preferred_element_type=F32)
    o_ref[...] = _layernorm(DEEPNORM_ALPHA * x_ref[...] + y, g_ref[...], b_ref[...])


def _post(mix, qm_src, qm_block, kv, w_out, x, g, b, seq, tm):
    t = x.shape[0]
    n_s = seq // tm
    return pl.pallas_call(
        _post_kernel,
        grid=(t // tm,),
        in_specs=[pl.BlockSpec((tm, MIX_WIDTH), lambda i: (i, 0)),
                  pl.BlockSpec((tm, MEM_WIDTH), lambda i: (i, qm_block)),
                  pl.BlockSpec((MEM_LEN, 2 * MEM_WIDTH), lambda i: (i // n_s, 0)),
                  _const_spec(w_out.shape),
                  pl.BlockSpec((tm, D_MODEL), lambda i: (i, 0)),
                  _const_spec(g.shape),
                  _const_spec(b.shape)],
        out_specs=pl.BlockSpec((tm, D_MODEL), lambda i: (i, 0)),
        out_shape=jax.ShapeDtypeStruct((t, D_MODEL), F32),
        scratch_shapes=[pltpu.VMEM((tm, D_MODEL), BF16)],
        compiler_params=_params(1),
        name="post_mix",
    )(mix, qm_src, kv, w_out, x, g, b)


def _ffn_kernel(x_ref, wg_ref, wu_ref, cwg_ref, cwu_ref, cbg_ref, cbu_ref, wd_ref, g_ref, b_ref, o_ref,
                xb_ref, acc_ref, hg_ref, hu_ref, carry_g_ref, carry_u_ref, *, tm, tiles_per_seq):
    i = pl.program_id(0)
    f = pl.program_id(1)
    n_f = pl.num_programs(1)
    seq_start = (i % tiles_per_seq) == 0

    @pl.when(f == 0)
    def _():
        xb_ref[...] = x_ref[...].astype(BF16)

    def conv_branch(w_ref, cw_ref, cb_ref, h_ref, carry_ref):
        h = jnp.dot(xb_ref[...], w_ref[...], preferred_element_type=F32)
        @pl.when(seq_start)
        def _():
            h_ref[0:CONV_HALO, :] = jnp.zeros((CONV_HALO, h.shape[1]), F32)

        @pl.when(jnp.logical_not(seq_start))
        def _():
            h_ref[0:CONV_HALO, :] = carry_ref[f]

        h_ref[CONV_HALO:CONV_HALO + tm, :] = h
        carry_ref[f] = h[tm - CONV_HALO:tm, :]
        return (cb_ref[...] + cw_ref[2:3, :] * h
                + cw_ref[1:2, :] * h_ref[CONV_HALO - 1:CONV_HALO - 1 + tm, :]
                + cw_ref[0:1, :] * h_ref[CONV_HALO - 2:CONV_HALO - 2 + tm, :])

    gate = conv_branch(wg_ref, cwg_ref, cbg_ref, hg_ref, carry_g_ref)
    up = conv_branch(wu_ref, cwu_ref, cbu_ref, hu_ref, carry_u_ref)
    act = gate * jax.nn.sigmoid(gate) * up
    contrib = jnp.dot(act.astype(BF16), wd_ref[...], preferred_element_type=F32)

    @pl.when(f == 0)
    def _():
        acc_ref[...] = contrib

    @pl.when(f > 0)
    def _():
        acc_ref[...] += contrib

    @pl.when(f == n_f - 1)
    def _():
        o_ref[...] = _layernorm(DEEPNORM_ALPHA * x_ref[...] + acc_ref[...], g_ref[...], b_ref[...])


def _ffn(x, w_up, conv_w, conv_b, w_down, g, b, seq, tm, tf):
    t = x.shape[0]
    n_f = D_FF // tf
    return pl.pallas_call(
        functools.partial(_ffn_kernel, tm=tm, tiles_per_seq=seq // tm),
        grid=(t // tm, n_f),
        in_specs=[pl.BlockSpec((tm, D_MODEL), lambda i, f: (i, 0)),
                  pl.BlockSpec((D_MODEL, tf), lambda i, f: (0, f)),
                  pl.BlockSpec((D_MODEL, tf), lambda i, f: (0, n_f + f)),
                  pl.BlockSpec((CONV_WIDTH, tf), lambda i, f: (0, f)),
                  pl.BlockSpec((CONV_WIDTH, tf), lambda i, f: (0, n_f + f)),
                  pl.BlockSpec((1, tf), lambda i, f: (0, f)),
                  pl.BlockSpec((1, tf), lambda i, f: (0, n_f + f)),
                  pl.BlockSpec((tf, D_MODEL), lambda i, f: (f, 0)),
                  _const_spec(g.shape),
                  _const_spec(b.shape)],
        out_specs=pl.BlockSpec((tm, D_MODEL), lambda i, f: (i, 0)),
        out_shape=jax.ShapeDtypeStruct((t, D_MODEL), F32),
        scratch_shapes=[pltpu.VMEM((tm, D_MODEL), BF16),
                        pltpu.VMEM((tm, D_MODEL), F32),
                        pltpu.VMEM((tm + CONV_HALO, tf), F32),
                        pltpu.VMEM((tm + CONV_HALO, tf), F32),
                        pltpu.VMEM((n_f, CONV_HALO, tf), F32),
                        pltpu.VMEM((n_f, CONV_HALO, tf), F32)],
        compiler_params=_params(2),
        name="conv_ffn",
    )(x, w_up, w_up, conv_w, conv_w, conv_b, conv_b, w_down, g, b)


def kernel(x, mem, positions, pool_w_in, pool_w, pool_scale, diff_w_in, diff_lambda_q1, diff_lambda_k1,
           diff_lambda_q2, diff_lambda_k2, diff_subln_g, mem_w_kv, w_out, ln1_g, ln1_b, ffn_w_up,
           ffn_conv_w, ffn_conv_b, ffn_w_down, ln2_g, ln2_b):
    batch, seq, _ = x.shape
    t = batch * seq
    row = lambda a: a.reshape(1, -1)

    xs = x.reshape(t, D_MODEL)
    mem2 = mem.reshape(batch * MEM_LEN, D_MODEL)
    pos = positions.reshape(t, 1)
    inv_freq = ROPE_THETA ** (-jnp.arange(0, ROPE_DIM, 2, dtype=F32) / ROPE_DIM)
    invf = jnp.tile(jnp.tile(inv_freq, DIFF_QK_DIM // ROPE_HALF), LANES // DIFF_QK_DIM).reshape(1, LANES)

    for i in range(DEPTH):
        j = i // 2
        kv = _matmul(mem2, mem_w_kv[i].astype(BF16), BF16, tm=batch * MEM_LEN, tn=512)
        if i % 2 == 0:
            h = _matmul(xs, pool_w_in[j].astype(BF16), F32, tm=1024, tn=1024)
            mix = _pool(h, pool_w[j].astype(BF16), row(pool_scale[j]), batch, seq, tm=512)
            qm_src, qm_block = h, MIX_WIDTH // MEM_WIDTH
        else:
            qt, kk, vt, qm = _inproj_rope(xs, diff_w_in[j].astype(BF16), pos, invf, batch, seq,
                                          tm=1024, tn=MEM_WIDTH, tk=ATTN_TILE)
            lam_init = 0.8 - 0.6 * math.exp(-0.3 * i)
            mix = _diff_attn(qt, kk.reshape(batch, seq, MIX_WIDTH), vt, row(diff_lambda_q1[j]),
                             row(diff_lambda_k1[j]), row(diff_lambda_q2[j]), row(diff_lambda_k2[j]),
                             row(diff_subln_g[j]), batch, seq, tq=ATTN_TILE, lam_init=lam_init)
            mix = mix.reshape(t, MIX_WIDTH)
            qm_src, qm_block = qm, 0
        xs = _post(mix, qm_src, qm_block, kv, w_out[i].astype(BF16), xs, row(ln1_g[i]), row(ln1_b[i]),
                   seq, tm=512)
        xs = _ffn(xs, ffn_w_up[i].astype(BF16), ffn_conv_w[i], row(ffn_conv_b[i]),
                  ffn_w_down[i].astype(BF16), row(ln2_g[i]), row(ln2_b[i]), seq, tm=512, tf=512)
    return xs.reshape(batch, seq, D_MODEL)
```

```python
import functools
import math

import jax
import jax.numpy as jnp
from jax import lax
from jax.experimental import pallas as pl
from jax.experimental.pallas import tpu as pltpu

D_MODEL = 2048
DEPTH = 2
MEM_LEN = 256
MEM_HEADS = 4
MEM_HEAD_DIM = 128
MEM_WIDTH = MEM_HEADS * MEM_HEAD_DIM
MIX_WIDTH = D_MODEL - MEM_WIDTH
POOL_WINDOWS = (2, 4, 8, 16)
POOL_GROUP_DIM = MIX_WIDTH // len(POOL_WINDOWS)
POOL_HALO = 16
DIFF_HEADS = 12
DIFF_QK_DIM = 64
DIFF_V_DIM = 2 * DIFF_QK_DIM
ROPE_DIM = DIFF_QK_DIM // 4
ROPE_HALF = ROPE_DIM // 2
ROPE_THETA = 500000.0
D_FF = 5632
CONV_WIDTH = 3
CONV_HALO = 8
LN_EPS = 1e-5
RMS_EPS = 1e-6
DEEPNORM_ALPHA = (2.0 * DEPTH) ** 0.25
DIFF_IN_WIDTH = 3 * MIX_WIDTH + MEM_WIDTH
MASK_VALUE = -1e30
QK_SCALE_LOG2E = DIFF_QK_DIM ** -0.5 * math.log2(math.e)

LANES = 128
MXU_WIDTH = 256
MIX_SLABS = MIX_WIDTH // LANES
FFN_SUB = MXU_WIDTH
ATTN_TILE = 512
VMEM_LIMIT = 56 * 1024 * 1024

BF16 = jnp.bfloat16
F32 = jnp.float32


def _params(n_axes):
    return pltpu.CompilerParams(dimension_semantics=("arbitrary",) * n_axes,
                                vmem_limit_bytes=VMEM_LIMIT)


def _const_spec(shape):
    return pl.BlockSpec(shape, lambda *_: (0,) * len(shape), pipeline_mode=pl.Buffered(1))


def _layernorm(z, g, b):
    mu = jnp.mean(z, axis=-1, keepdims=True)
    zc = z - mu
    var = jnp.mean(zc * zc, axis=-1, keepdims=True)
    return zc * lax.rsqrt(var + LN_EPS) * g + b


def _matmul_kernel(x_ref, w_ref, o_ref, xb_ref):
    @pl.when(pl.program_id(1) == 0)
    def _():
        xb_ref[...] = x_ref[...].astype(BF16)

    o_ref[...] = jnp.dot(xb_ref[...], w_ref[...], preferred_element_type=F32).astype(o_ref.dtype)


def _matmul(x, w, out_dtype, tm, tn):
    m, k = x.shape
    n = w.shape[1]
    return pl.pallas_call(
        _matmul_kernel,
        grid=(m // tm, n // tn),
        in_specs=[pl.BlockSpec((tm, k), lambda i, j: (i, 0)),
                  pl.BlockSpec((k, tn), lambda i, j: (0, j))],
        out_specs=pl.BlockSpec((tm, tn), lambda i, j: (i, j)),
        out_shape=jax.ShapeDtypeStruct((m, n), out_dtype),
        scratch_shapes=[pltpu.VMEM((tm, k), BF16)],
        compiler_params=_params(2),
        name="matmul",
    )(x, w)


def _inproj_rope_kernel(x_ref, w_ref, pos_ref, invf_ref, qt_ref, k_ref, vt_ref, qm_ref, xb_ref, tab_ref,
                        *, tm, tn, tk):
    j = pl.program_id(1)
    n_qk_tiles = 2 * MIX_WIDTH // tn
    n_q_tiles = MIX_WIDTH // tn

    @pl.when(j == 0)
    def _():
        xb_ref[...] = x_ref[...].astype(BF16)
        ang = pos_ref[...].astype(F32) * invf_ref[...]
        cos = jnp.cos(ang)
        sin = jnp.sin(ang)
        r = lax.broadcasted_iota(jnp.int32, ang.shape, 1) % DIFF_QK_DIM
        tab_ref[0] = jnp.where(r < ROPE_DIM, cos, 1.0)
        tab_ref[1] = jnp.where(r < ROPE_HALF, -sin, 0.0)
        tab_ref[2] = jnp.where((r >= ROPE_HALF) & (r < ROPE_DIM), sin, 0.0)

    acc = jnp.dot(xb_ref[...], w_ref[...], preferred_element_type=F32)

    def rope(a):
        up = pltpu.roll(a, LANES - ROPE_HALF, 1)
        dn = pltpu.roll(a, ROPE_HALF, 1)
        return a * tab_ref[0] + up * tab_ref[1] + dn * tab_ref[2]

    heads = tn // LANES

    @pl.when(j < n_q_tiles)
    def _():
        for c in range(heads):
            qt = (rope(acc[:, c * LANES:(c + 1) * LANES]) * QK_SCALE_LOG2E).T
            for s in range(tm // tk):
                qt_ref[c, s] = qt[:, s * tk:(s + 1) * tk].astype(BF16)

    @pl.when((j >= n_q_tiles) & (j < n_qk_tiles))
    def _():
        for c in range(heads):
            k_ref[c] = rope(acc[:, c * LANES:(c + 1) * LANES]).astype(BF16)

    @pl.when((j >= n_qk_tiles) & (j < n_qk_tiles + n_q_tiles))
    def _():
        for c in range(heads):
            vt = acc[:, c * LANES:(c + 1) * LANES].T
            for s in range(tm // tk):
                vt_ref[c, s] = vt[:, s * tk:(s + 1) * tk].astype(BF16)

    @pl.when(j >= n_qk_tiles + n_q_tiles)
    def _():
        qm_ref[...] = acc.astype(BF16)


def _inproj_rope(x, w, pos, invf, batch, seq, tm, tn, tk):
    m, k = x.shape
    n = w.shape[1]
    assert tn == MEM_WIDTH and MIX_WIDTH % tn == 0
    n_s = seq // tm
    nq = MIX_WIDTH // tn
    clamp = lambda j, lo: jnp.clip(j - lo, 0, nq - 1)
    heads = tn // LANES
    return pl.pallas_call(
        functools.partial(_inproj_rope_kernel, tm=tm, tn=tn, tk=tk),
        grid=(m // tm, n // tn),
        in_specs=[pl.BlockSpec((tm, k), lambda i, j: (i, 0)),
                  pl.BlockSpec((k, tn), lambda i, j: (0, j)),
                  pl.BlockSpec((tm, 1), lambda i, j: (i, 0)),
                  pl.BlockSpec((1, LANES), lambda i, j: (0, 0))],
        out_specs=[pl.BlockSpec((None, heads, tm // tk, LANES, tk),
                                lambda i, j: (i // n_s, clamp(j, 0), i % n_s, 0, 0)),
                   pl.BlockSpec((None, heads, tm, LANES), lambda i, j: (i // n_s, clamp(j, nq), i % n_s, 0)),
                   pl.BlockSpec((None, heads, tm // tk, LANES, tk),
                                lambda i, j: (i // n_s, clamp(j, 2 * nq), i % n_s, 0, 0)),
                   pl.BlockSpec((tm, tn), lambda i, j: (i, 0))],
        out_shape=[jax.ShapeDtypeStruct((batch, DIFF_HEADS, seq // tk, LANES, tk), BF16),
                   jax.ShapeDtypeStruct((batch, DIFF_HEADS, seq, LANES), BF16),
                   jax.ShapeDtypeStruct((batch, DIFF_HEADS, seq // tk, LANES, tk), BF16),
                   jax.ShapeDtypeStruct((m, MEM_WIDTH), BF16)],
        scratch_shapes=[pltpu.VMEM((tm, k), BF16), pltpu.VMEM((3, tm, LANES), F32)],
        compiler_params=_params(2),
        name="inproj_rope",
    )(x, w, pos, invf)


def _pool_kernel(h_ref, w_ref, scale_ref, o_ref, ext_ref, *, tm):
    i = pl.program_id(1)

    @pl.when(i == 0)
    def _():
        ext_ref[0:POOL_HALO, :] = jnp.zeros((POOL_HALO, MIX_WIDTH), F32)

    @pl.when(i > 0)
    def _():
        ext_ref[0:POOL_HALO, :] = ext_ref[tm:tm + POOL_HALO, :]

    ext_ref[POOL_HALO:POOL_HALO + tm, :] = h_ref[...]

    pos = i * tm + lax.broadcasted_iota(jnp.int32, (tm, 1), 0)
    for g, win in enumerate(POOL_WINDOWS):
        cols = slice(g * POOL_GROUP_DIM, (g + 1) * POOL_GROUP_DIM)
        cur = ext_ref[POOL_HALO:POOL_HALO + tm, cols]
        tot = cur
        for lag in range(1, win):
            tot = tot + ext_ref[POOL_HALO - lag:POOL_HALO - lag + tm, cols]
        cnt = jnp.minimum(pos + 1, win).astype(F32)
        d = tot / cnt - cur
        y = jnp.dot(d.astype(BF16), w_ref[g], preferred_element_type=F32) * scale_ref[:, cols]
        slabs = POOL_GROUP_DIM // LANES
        for c in range(slabs):
            o_ref[g * slabs + c] = y[:, c * LANES:(c + 1) * LANES].astype(o_ref.dtype)


def _pool(h, w_groups, scale, batch, seq, tm):
    n_s = seq // tm
    return pl.pallas_call(
        functools.partial(_pool_kernel, tm=tm),
        grid=(batch, n_s),
        in_specs=[pl.BlockSpec((tm, MIX_WIDTH), lambda b, i: (b * n_s + i, 0)),
                  _const_spec(w_groups.shape),
                  _const_spec(scale.shape)],
        out_specs=pl.BlockSpec((None, MIX_SLABS, tm, LANES), lambda b, i: (b, 0, i, 0)),
        out_shape=jax.ShapeDtypeStruct((batch, MIX_SLABS, seq, LANES), BF16),
        scratch_shapes=[pltpu.VMEM((tm + POOL_HALO, MIX_WIDTH), F32)],
        compiler_params=_params(2),
        name="pool_mixer",
    )(h, w_groups, scale)


def _diff_attn_kernel(lq1_ref, lk1_ref, lq2_ref, lk2_ref, g_ref, qt_ref, k_ref, vt_ref, o_ref,
                      w_ref, sa_ref, sb_ref, m_ref, l_ref, acc_ref, *, tq, lam_init):
    i = pl.program_id(2)
    tk = tq

    qt = qt_ref[...]
    row = lax.broadcasted_iota(jnp.int32, qt.shape, 0)
    zero = jnp.zeros_like(qt)
    w_ref[:, 0:tq] = jnp.where(row < DIFF_QK_DIM, qt, zero)
    w_ref[:, tq:2 * tq] = jnp.where(row >= DIFF_QK_DIM, qt, zero)
    m_ref[...] = jnp.full(m_ref.shape, MASK_VALUE, F32)
    l_ref[...] = jnp.zeros(l_ref.shape, F32)
    acc_ref[...] = jnp.zeros(acc_ref.shape, F32)

    def scores(j, s_ref):
        start = pl.multiple_of(j * tk, tk)
        s_ref[...] = jnp.dot(k_ref[pl.ds(start, tk), :], w_ref[...], preferred_element_type=F32)

    def softmax_pv(j, s_ref, masked):
        s = s_ref[...]
        if masked:
            kpos = lax.broadcasted_iota(jnp.int32, s.shape, 0)
            qpos = lax.broadcasted_iota(jnp.int32, s.shape, 1) % tq
            s = jnp.where(kpos <= qpos, s, MASK_VALUE)
        m_old = m_ref[...]
        m_new = jnp.maximum(m_old, jnp.max(s, axis=0, keepdims=True))
        alpha = jnp.exp2(m_old - m_new)
        p = jnp.exp2(s - m_new)
        l_ref[...] = alpha * l_ref[...] + jnp.sum(p, axis=0, keepdims=True)
        acc_ref[...] = alpha * acc_ref[...] + jnp.dot(vt_ref[j], p.astype(BF16), preferred_element_type=F32)
        m_ref[...] = m_new

    scores(0, sa_ref)

    def pair(jj, carry):
        j = 2 * jj
        scores(j + 1, sb_ref)
        softmax_pv(j, sa_ref, masked=False)
        scores(j + 2, sa_ref)
        softmax_pv(j + 1, sb_ref, masked=False)
        return carry

    lax.fori_loop(0, i // 2, pair, 0)

    @pl.when(i % 2 == 0)
    def _():
        softmax_pv(i, sa_ref, masked=True)

    @pl.when(i % 2 == 1)
    def _():
        scores(i, sb_ref)
        softmax_pv(i - 1, sa_ref, masked=False)
        softmax_pv(i, sb_ref, masked=True)

    lam = (jnp.exp(jnp.sum(lq1_ref[...] * lk1_ref[...], axis=-1, keepdims=True))
           - jnp.exp(jnp.sum(lq2_ref[...] * lk2_ref[...], axis=-1, keepdims=True)) + lam_init)
    o1 = acc_ref[:, 0:tq] / l_ref[:, 0:tq]
    o2 = acc_ref[:, tq:2 * tq] / l_ref[:, tq:2 * tq]
    ot = o1 - lam * o2
    ot = ot * lax.rsqrt(jnp.mean(ot * ot, axis=0, keepdims=True) + RMS_EPS)
    o_ref[...] = (ot.T * g_ref[...] * (1.0 - lam_init)).astype(o_ref.dtype)


def _diff_attn(qt, k, vt, lq1, lk1, lq2, lk2, g, batch, seq, tq, lam_init):
    n_q = seq // tq
    vec = lambda n: pl.BlockSpec((1, n), lambda b, h, i: (0, 0))
    return pl.pallas_call(
        functools.partial(_diff_attn_kernel, tq=tq, lam_init=lam_init),
        grid=(batch, DIFF_HEADS, n_q),
        in_specs=[vec(DIFF_QK_DIM), vec(DIFF_QK_DIM), vec(DIFF_QK_DIM), vec(DIFF_QK_DIM), vec(DIFF_V_DIM),
                  pl.BlockSpec((None, None, None, LANES, tq), lambda b, h, i: (b, h, i, 0, 0)),
                  pl.BlockSpec((None, None, seq, LANES), lambda b, h, i: (b, h, 0, 0)),
                  pl.BlockSpec((None, None, n_q, LANES, tq), lambda b, h, i: (b, h, 0, 0, 0))],
        out_specs=pl.BlockSpec((None, None, tq, LANES), lambda b, h, i: (b, h, i, 0)),
        out_shape=jax.ShapeDtypeStruct((batch, DIFF_HEADS, seq, LANES), BF16),
        scratch_shapes=[pltpu.VMEM((LANES, 2 * tq), BF16),
                        pltpu.VMEM((tq, 2 * tq), F32),
                        pltpu.VMEM((tq, 2 * tq), F32),
                        pltpu.VMEM((1, 2 * tq), F32),
                        pltpu.VMEM((1, 2 * tq), F32),
                        pltpu.VMEM((DIFF_V_DIM, 2 * tq), F32)],
        compiler_params=_params(3),
        name="diff_attn",
    )(lq1, lk1, lq2, lk2, g, qt, k, vt)


def _post_kernel(mix_ref, qm_ref, kv_ref, w_ref, x_ref, g_ref, b_ref, o_ref, cat_ref):
    for c in range(MIX_SLABS):
        cat_ref[:, c * LANES:(c + 1) * LANES] = mix_ref[c]
    q = qm_ref[...].astype(BF16)
    for hd in range(MEM_HEADS):
        lo = hd * MEM_HEAD_DIM
        qh = q[:, lo:lo + MEM_HEAD_DIM]
        kh = kv_ref[:, lo:lo + MEM_HEAD_DIM]
        vh = kv_ref[:, MEM_WIDTH + lo:MEM_WIDTH + lo + MEM_HEAD_DIM]
        s = lax.dot_general(qh, kh, (((1,), (1,)), ((), ())), preferred_element_type=F32)
        s = s * (MEM_HEAD_DIM ** -0.5)
        p = jnp.exp(s - jnp.max(s, axis=-1, keepdims=True))
        denom = jnp.sum(p, axis=-1, keepdims=True)
        oh = jnp.dot(p.astype(BF16), vh, preferred_element_type=F32) / denom
        cat_ref[:, MIX_WIDTH + lo:MIX_WIDTH + lo + MEM_HEAD_DIM] = oh.astype(BF16)
    y = jnp.dot(cat_ref[...], w_ref[...], preferred_element_type=F32)
    o_ref[...] = _layernorm(DEEPNORM_ALPHA * x_ref[...] + y, g_ref[...], b_ref[...])


def _post(mix, qm_src, qm_block, kv, w_out, x, g, b, seq, tm):
    t = x.shape[0]
    n_s = seq // tm
    return pl.pallas_call(
        _post_kernel,
        grid=(t // tm,),
        in_specs=[pl.BlockSpec((None, MIX_SLABS, tm, LANES), lambda i: (i // n_s, 0, i % n_s, 0)),
                  pl.BlockSpec((tm, MEM_WIDTH), lambda i: (i, qm_block)),
                  pl.BlockSpec((MEM_LEN, 2 * MEM_WIDTH), lambda i: (i // n_s, 0)),
                  _const_spec(w_out.shape),
                  pl.BlockSpec((tm, D_MODEL), lambda i: (i, 0)),
                  _const_spec(g.shape),
                  _const_spec(b.shape)],
        out_specs=pl.BlockSpec((tm, D_MODEL), lambda i: (i, 0)),
        out_shape=jax.ShapeDtypeStruct((t, D_MODEL), F32),
        scratch_shapes=[pltpu.VMEM((tm, D_MODEL), BF16)],
        compiler_params=_params(1),
        name="post_mix",
    )(mix, qm_src, kv, w_out, x, g, b)


def _ffn_kernel(x_ref, wg_ref, wu_ref, cwg_ref, cwu_ref, cbg_ref, cbu_ref, wd_ref, g_ref, b_ref, o_ref,
                xb_ref, carry_g_ref, carry_u_ref, *h_refs, tm, tf, tiles_per_seq):
    i = pl.program_id(0)
    f = pl.program_id(1)
    n_f = pl.num_programs(1)
    seq_start = (i % tiles_per_seq) == 0

    @pl.when(f == 0)
    def _():
        xb_ref[...] = x_ref[...].astype(BF16)
        o_ref[...] = jnp.zeros(o_ref.shape, F32)

    @pl.when((f == 0) & (i == 0))
    def _():
        carry_g_ref[...] = jnp.zeros(carry_g_ref.shape, F32)
        carry_u_ref[...] = jnp.zeros(carry_u_ref.shape, F32)

    def conv_branch(cols, w_ref, cw_ref, cb_ref, h_ref, carry_ref):
        h = jnp.dot(xb_ref[...], w_ref[:, cols], preferred_element_type=F32)
        prev = carry_ref[f, :, cols]
        h_ref[0:CONV_HALO, :] = jnp.where(seq_start, jnp.zeros_like(prev), prev)
        h_ref[CONV_HALO:CONV_HALO + tm, :] = h
        carry_ref[f, :, cols] = h[tm - CONV_HALO:tm, :]
        return (cb_ref[:, cols] + cw_ref[2:3, cols] * h
                + cw_ref[1:2, cols] * h_ref[CONV_HALO - 1:CONV_HALO - 1 + tm, :]
                + cw_ref[0:1, cols] * h_ref[CONV_HALO - 2:CONV_HALO - 2 + tm, :])

    total = None
    for c in range(tf // FFN_SUB):
        cols = slice(c * FFN_SUB, (c + 1) * FFN_SUB)
        gate = conv_branch(cols, wg_ref, cwg_ref, cbg_ref, h_refs[2 * c], carry_g_ref)
        up = conv_branch(cols, wu_ref, cwu_ref, cbu_ref, h_refs[2 * c + 1], carry_u_ref)
        act = gate * jax.nn.sigmoid(gate) * up
        part = jnp.dot(act.astype(BF16), wd_ref[cols, :], preferred_element_type=F32)
        total = part if total is None else total + part
    o_ref[...] += total

    @pl.when(f == n_f - 1)
    def _():
        o_ref[...] = _layernorm(DEEPNORM_ALPHA * x_ref[...] + o_ref[...], g_ref[...], b_ref[...])


def _ffn(x, w_up, conv_w, conv_b, w_down, g, b, seq, tm, tf):
    t = x.shape[0]
    n_f = D_FF // tf
    return pl.pallas_call(
        functools.partial(_ffn_kernel, tm=tm, tf=tf, tiles_per_seq=seq // tm),
        grid=(t // tm, n_f),
        in_specs=[pl.BlockSpec((tm, D_MODEL), lambda i, f: (i, 0)),
                  pl.BlockSpec((D_MODEL, tf), lambda i, f: (0, f)),
                  pl.BlockSpec((D_MODEL, tf), lambda i, f: (0, n_f + f)),
                  pl.BlockSpec((CONV_WIDTH, tf), lambda i, f: (0, f)),
                  pl.BlockSpec((CONV_WIDTH, tf), lambda i, f: (0, n_f + f)),
                  pl.BlockSpec((1, tf), lambda i, f: (0, f)),
                  pl.BlockSpec((1, tf), lambda i, f: (0, n_f + f)),
                  pl.BlockSpec((tf, D_MODEL), lambda i, f: (f, 0)),
                  _const_spec(g.shape),
                  _const_spec(b.shape)],
        out_specs=pl.BlockSpec((tm, D_MODEL), lambda i, f: (i, 0)),
        out_shape=jax.ShapeDtypeStruct((t, D_MODEL), F32),
        scratch_shapes=[pltpu.VMEM((tm, D_MODEL), BF16),
                        pltpu.VMEM((n_f, CONV_HALO, tf), F32),
                        pltpu.VMEM((n_f, CONV_HALO, tf), F32)]
                       + [pltpu.VMEM((tm + CONV_HALO, FFN_SUB), F32)] * (2 * (tf // FFN_SUB)),
        compiler_params=_params(2),
        name="conv_ffn",
    )(x, w_up, w_up, conv_w, conv_w, conv_b, conv_b, w_down, g, b)


def kernel(x, mem, positions, pool_w_in, pool_w, pool_scale, diff_w_in, diff_lambda_q1, diff_lambda_k1,
           diff_lambda_q2, diff_lambda_k2, diff_subln_g, mem_w_kv, w_out, ln1_g, ln1_b, ffn_w_up,
           ffn_conv_w, ffn_conv_b, ffn_w_down, ln2_g, ln2_b):
    batch, seq, _ = x.shape
    t = batch * seq
    row = lambda a: a.reshape(1, -1)

    xs = x.reshape(t, D_MODEL)
    mem2 = mem.reshape(batch * MEM_LEN, D_MODEL)
    pos = positions.reshape(t, 1)
    inv_freq = ROPE_THETA ** (-jnp.arange(0, ROPE_DIM, 2, dtype=F32) / ROPE_DIM)
    invf = jnp.tile(jnp.tile(inv_freq, DIFF_QK_DIM // ROPE_HALF), LANES // DIFF_QK_DIM).reshape(1, LANES)

    for i in range(DEPTH):
        j = i // 2
        kv = _matmul(mem2, mem_w_kv[i].astype(BF16), BF16, tm=batch * MEM_LEN, tn=512)
        if i % 2 == 0:
            h = _matmul(xs, pool_w_in[j].astype(BF16), F32, tm=1024, tn=1024)
            mix = _pool(h, pool_w[j].astype(BF16), row(pool_scale[j]), batch, seq, tm=512)
            qm_src, qm_block = h, MIX_WIDTH // MEM_WIDTH
        else:
            qt, kk, vt, qm = _inproj_rope(xs, diff_w_in[j].astype(BF16), pos, invf, batch, seq,
                                          tm=1024, tn=MEM_WIDTH, tk=ATTN_TILE)
            lam_init = 0.8 - 0.6 * math.exp(-0.3 * i)
            mix = _diff_attn(qt, kk, vt, row(diff_lambda_q1[j]),
                             row(diff_lambda_k1[j]), row(diff_lambda_q2[j]), row(diff_lambda_k2[j]),
                             row(diff_subln_g[j]), batch, seq, tq=ATTN_TILE, lam_init=lam_init)
            qm_src, qm_block = qm, 0
        xs = _post(mix, qm_src, qm_block, kv, w_out[i].astype(BF16), xs, row(ln1_g[i]), row(ln1_b[i]),
                   seq, tm=512)
        xs = _ffn(xs, ffn_w_up[i].astype(BF16), ffn_conv_w[i], row(ffn_conv_b[i]),
                  ffn_w_down[i].astype(BF16), row(ln2_g[i]), row(ln2_b[i]), seq, tm=512, tf=512)
    return xs.reshape(batch, seq, D_MODEL)
```

```python
import functools
import math

import jax
import jax.numpy as jnp
from jax import lax
from jax.experimental import pallas as pl
from jax.experimental.pallas import tpu as pltpu

D_MODEL = 2048
DEPTH = 2
MEM_LEN = 256
MEM_HEADS = 4
MEM_HEAD_DIM = 128
MEM_WIDTH = MEM_HEADS * MEM_HEAD_DIM
MIX_WIDTH = D_MODEL - MEM_WIDTH
POOL_WINDOWS = (2, 4, 8, 16)
POOL_GROUP_DIM = MIX_WIDTH // len(POOL_WINDOWS)
POOL_HALO = 16
DIFF_HEADS = 12
DIFF_QK_DIM = 64
DIFF_V_DIM = 2 * DIFF_QK_DIM
VT_ROWS = DIFF_V_DIM + 16
ROPE_DIM = DIFF_QK_DIM // 4
ROPE_HALF = ROPE_DIM // 2
ROPE_THETA = 500000.0
D_FF = 5632
CONV_WIDTH = 3
CONV_HALO = 8
LN_EPS = 1e-5
RMS_EPS = 1e-6
DEEPNORM_ALPHA = (2.0 * DEPTH) ** 0.25
DIFF_IN_WIDTH = 3 * MIX_WIDTH + MEM_WIDTH
MASK_VALUE = -1e30
QK_SCALE_LOG2E = DIFF_QK_DIM ** -0.5 * math.log2(math.e)

LANES = 128
MXU_WIDTH = 256
MIX_SLABS = MIX_WIDTH // LANES
FFN_SUB = MXU_WIDTH
ATTN_TILE = 512
VMEM_LIMIT = 56 * 1024 * 1024

BF16 = jnp.bfloat16
F32 = jnp.float32


def _params(n_axes):
    return pltpu.CompilerParams(dimension_semantics=("arbitrary",) * n_axes,
                                vmem_limit_bytes=VMEM_LIMIT)


def _const_spec(shape):
    return pl.BlockSpec(shape, lambda *_: (0,) * len(shape), pipeline_mode=pl.Buffered(1))


def _layernorm(z, g, b):
    mu = jnp.mean(z, axis=-1, keepdims=True)
    zc = z - mu
    var = jnp.mean(zc * zc, axis=-1, keepdims=True)
    return zc * lax.rsqrt(var + LN_EPS) * g + b


def _matmul_kernel(x_ref, w_ref, o_ref, xb_ref):
    @pl.when(pl.program_id(1) == 0)
    def _():
        xb_ref[...] = x_ref[...].astype(BF16)

    o_ref[...] = jnp.dot(xb_ref[...], w_ref[...], preferred_element_type=F32).astype(o_ref.dtype)


def _matmul(x, w, out_dtype, tm, tn):
    m, k = x.shape
    n = w.shape[1]
    return pl.pallas_call(
        _matmul_kernel,
        grid=(m // tm, n // tn),
        in_specs=[pl.BlockSpec((tm, k), lambda i, j: (i, 0)),
                  pl.BlockSpec((k, tn), lambda i, j: (0, j))],
        out_specs=pl.BlockSpec((tm, tn), lambda i, j: (i, j)),
        out_shape=jax.ShapeDtypeStruct((m, n), out_dtype),
        scratch_shapes=[pltpu.VMEM((tm, k), BF16)],
        compiler_params=_params(2),
        name="matmul",
    )(x, w)


def _inproj_rope_kernel(x_ref, w_ref, pos_ref, invf_ref, qt_ref, k_ref, vt_ref, qm_ref, xb_ref, tab_ref,
                        *, tm, tn, tk):
    j = pl.program_id(1)
    n_qk_tiles = 2 * MIX_WIDTH // tn
    n_q_tiles = MIX_WIDTH // tn

    @pl.when(j == 0)
    def _():
        xb_ref[...] = x_ref[...].astype(BF16)
        ang = pos_ref[...].astype(F32) * invf_ref[...]
        cos = jnp.cos(ang)
        sin = jnp.sin(ang)
        r = lax.broadcasted_iota(jnp.int32, ang.shape, 1) % DIFF_QK_DIM
        tab_ref[0] = jnp.where(r < ROPE_DIM, cos, 1.0)
        tab_ref[1] = jnp.where(r < ROPE_HALF, -sin, 0.0)
        tab_ref[2] = jnp.where((r >= ROPE_HALF) & (r < ROPE_DIM), sin, 0.0)

    def rope(a):
        up = pltpu.roll(a, LANES - ROPE_HALF, 1)
        dn = pltpu.roll(a, ROPE_HALF, 1)
        return a * tab_ref[0] + up * tab_ref[1] + dn * tab_ref[2]

    def head_slabs():
        per = MXU_WIDTH // LANES
        for piece in range(tn // MXU_WIDTH):
            acc = jnp.dot(xb_ref[...], w_ref[:, piece * MXU_WIDTH:(piece + 1) * MXU_WIDTH],
                          preferred_element_type=F32)
            for c in range(per):
                yield piece * per + c, acc[:, c * LANES:(c + 1) * LANES]

    @pl.when(j < n_q_tiles)
    def _():
        for c, a in head_slabs():
            qt = (rope(a) * QK_SCALE_LOG2E).T
            for s in range(tm // tk):
                qt_ref[c, s] = qt[:, s * tk:(s + 1) * tk].astype(BF16)

    @pl.when((j >= n_q_tiles) & (j < n_qk_tiles))
    def _():
        for c, a in head_slabs():
            k_ref[c] = rope(a).astype(BF16)

    @pl.when((j >= n_qk_tiles) & (j < n_qk_tiles + n_q_tiles))
    def _():
        for c, a in head_slabs():
            vt = a.T
            for s in range(tm // tk):
                vt_ref[c, s, 0:DIFF_V_DIM, :] = vt[:, s * tk:(s + 1) * tk].astype(BF16)
                vt_ref[c, s, DIFF_V_DIM:VT_ROWS, :] = jnp.ones((VT_ROWS - DIFF_V_DIM, tk), BF16)

    @pl.when(j >= n_qk_tiles + n_q_tiles)
    def _():
        for c, a in head_slabs():
            qm_ref[:, c * LANES:(c + 1) * LANES] = a.astype(BF16)


def _inproj_rope(x, w, pos, invf, batch, seq, tm, tn, tk):
    m, k = x.shape
    n = w.shape[1]
    assert tn == MEM_WIDTH and MIX_WIDTH % tn == 0
    n_s = seq // tm
    nq = MIX_WIDTH // tn
    clamp = lambda j, lo: jnp.clip(j - lo, 0, nq - 1)
    heads = tn // LANES
    return pl.pallas_call(
        functools.partial(_inproj_rope_kernel, tm=tm, tn=tn, tk=tk),
        grid=(m // tm, n // tn),
        in_specs=[pl.BlockSpec((tm, k), lambda i, j: (i, 0)),
                  pl.BlockSpec((k, tn), lambda i, j: (0, j)),
                  pl.BlockSpec((tm, 1), lambda i, j: (i, 0)),
                  pl.BlockSpec((1, LANES), lambda i, j: (0, 0))],
        out_specs=[pl.BlockSpec((None, heads, tm // tk, LANES, tk),
                                lambda i, j: (i // n_s, clamp(j, 0), i % n_s, 0, 0)),
                   pl.BlockSpec((None, heads, tm, LANES), lambda i, j: (i // n_s, clamp(j, nq), i % n_s, 0)),
                   pl.BlockSpec((None, heads, tm // tk, VT_ROWS, tk),
                                lambda i, j: (i // n_s, clamp(j, 2 * nq), i % n_s, 0, 0)),
                   pl.BlockSpec((tm, tn), lambda i, j: (i, 0))],
        out_shape=[jax.ShapeDtypeStruct((batch, DIFF_HEADS, seq // tk, LANES, tk), BF16),
                   jax.ShapeDtypeStruct((batch, DIFF_HEADS, seq, LANES), BF16),
                   jax.ShapeDtypeStruct((batch, DIFF_HEADS, seq // tk, VT_ROWS, tk), BF16),
                   jax.ShapeDtypeStruct((m, MEM_WIDTH), BF16)],
        scratch_shapes=[pltpu.VMEM((tm, k), BF16), pltpu.VMEM((3, tm, LANES), F32)],
        compiler_params=_params(2),
        name="inproj_rope",
    )(x, w, pos, invf)


def _pool_kernel(h_ref, w_ref, scale_ref, o_ref, ext_ref, *, tm):
    i = pl.program_id(1)

    @pl.when(i == 0)
    def _():
        ext_ref[0:POOL_HALO, :] = jnp.zeros((POOL_HALO, MIX_WIDTH), F32)

    @pl.when(i > 0)
    def _():
        ext_ref[0:POOL_HALO, :] = ext_ref[tm:tm + POOL_HALO, :]

    ext_ref[POOL_HALO:POOL_HALO + tm, :] = h_ref[...]

    pos = i * tm + lax.broadcasted_iota(jnp.int32, (tm, 1), 0)
    for g, win in enumerate(POOL_WINDOWS):
        cols = slice(g * POOL_GROUP_DIM, (g + 1) * POOL_GROUP_DIM)
        cur = ext_ref[POOL_HALO:POOL_HALO + tm, cols]
        tot = cur
        for lag in range(1, win):
            tot = tot + ext_ref[POOL_HALO - lag:POOL_HALO - lag + tm, cols]
        cnt = jnp.minimum(pos + 1, win).astype(F32)
        d = tot / cnt - cur
        y = jnp.dot(d.astype(BF16), w_ref[g], preferred_element_type=F32) * scale_ref[:, cols]
        slabs = POOL_GROUP_DIM // LANES
        for c in range(slabs):
            o_ref[g * slabs + c] = y[:, c * LANES:(c + 1) * LANES].astype(o_ref.dtype)


def _pool(h, w_groups, scale, batch, seq, tm):
    n_s = seq // tm
    return pl.pallas_call(
        functools.partial(_pool_kernel, tm=tm),
        grid=(batch, n_s),
        in_specs=[pl.BlockSpec((tm, MIX_WIDTH), lambda b, i: (b * n_s + i, 0)),
                  _const_spec(w_groups.shape),
                  _const_spec(scale.shape)],
        out_specs=pl.BlockSpec((None, MIX_SLABS, tm, LANES), lambda b, i: (b, 0, i, 0)),
        out_shape=jax.ShapeDtypeStruct((batch, MIX_SLABS, seq, LANES), BF16),
        scratch_shapes=[pltpu.VMEM((tm + POOL_HALO, MIX_WIDTH), F32)],
        compiler_params=_params(2),
        name="pool_mixer",
    )(h, w_groups, scale)


def _diff_attn_kernel(lq1_ref, lk1_ref, lq2_ref, lk2_ref, g_ref, qt_ref, k_ref, vt_ref, o_ref,
                      w_ref, sa_ref, sb_ref, m_ref, acc_ref, *, tq, lam_init):
    i = pl.program_id(2)
    tk = tq

    qt = qt_ref[...]
    row = lax.broadcasted_iota(jnp.int32, qt.shape, 0)
    zero = jnp.zeros_like(qt)
    w_ref[:, 0:tq] = jnp.where(row < DIFF_QK_DIM, qt, zero)
    w_ref[:, tq:2 * tq] = jnp.where(row >= DIFF_QK_DIM, qt, zero)
    m_ref[...] = jnp.full(m_ref.shape, MASK_VALUE, F32)
    acc_ref[...] = jnp.zeros(acc_ref.shape, F32)

    def scores(j, s_ref):
        start = pl.multiple_of(j * tk, tk)
        s_ref[...] = jnp.dot(k_ref[pl.ds(start, tk), :], w_ref[...], preferred_element_type=F32)

    def softmax_pv(j, s_ref, masked):
        s = s_ref[...]
        if masked:
            kpos = lax.broadcasted_iota(jnp.int32, s.shape, 0)
            qpos = lax.broadcasted_iota(jnp.int32, s.shape, 1) % tq
            s = jnp.where(kpos <= qpos, s, MASK_VALUE)
        m_old = m_ref[...]
        m_new = jnp.maximum(m_old, jnp.max(s, axis=0, keepdims=True))
        alpha = jnp.exp2(m_old - m_new)
        p = jnp.exp2(s - m_new).astype(BF16)
        acc_ref[...] = alpha * acc_ref[...] + jnp.dot(vt_ref[j], p, preferred_element_type=F32)
        m_ref[...] = m_new

    scores(0, sa_ref)

    def pair(jj, carry):
        j = 2 * jj
        scores(j + 1, sb_ref)
        softmax_pv(j, sa_ref, masked=False)
        scores(j + 2, sa_ref)
        softmax_pv(j + 1, sb_ref, masked=False)
        return carry

    lax.fori_loop(0, i // 2, pair, 0)

    @pl.when(i % 2 == 0)
    def _():
        softmax_pv(i, sa_ref, masked=True)

    @pl.when(i % 2 == 1)
    def _():
        scores(i, sb_ref)
        softmax_pv(i - 1, sa_ref, masked=False)
        softmax_pv(i, sb_ref, masked=True)

    lam = (jnp.exp(jnp.sum(lq1_ref[...] * lk1_ref[...], axis=-1, keepdims=True))
           - jnp.exp(jnp.sum(lq2_ref[...] * lk2_ref[...], axis=-1, keepdims=True)) + lam_init)
    norm = acc_ref[DIFF_V_DIM:DIFF_V_DIM + 1, :]
    o1 = acc_ref[0:DIFF_V_DIM, 0:tq] / norm[:, 0:tq]
    o2 = acc_ref[0:DIFF_V_DIM, tq:2 * tq] / norm[:, tq:2 * tq]
    ot = o1 - lam * o2
    ot = ot * lax.rsqrt(jnp.mean(ot * ot, axis=0, keepdims=True) + RMS_EPS)
    o_ref[...] = (ot.T * g_ref[...] * (1.0 - lam_init)).astype(o_ref.dtype)


def _diff_attn(qt, k, vt, lq1, lk1, lq2, lk2, g, batch, seq, tq, lam_init):
    n_q = seq // tq
    vec = lambda n: pl.BlockSpec((1, n), lambda b, h, i: (0, 0))
    return pl.pallas_call(
        functools.partial(_diff_attn_kernel, tq=tq, lam_init=lam_init),
        grid=(batch, DIFF_HEADS, n_q),
        in_specs=[vec(DIFF_QK_DIM), vec(DIFF_QK_DIM), vec(DIFF_QK_DIM), vec(DIFF_QK_DIM), vec(DIFF_V_DIM),
                  pl.BlockSpec((None, None, None, LANES, tq), lambda b, h, i: (b, h, i, 0, 0)),
                  pl.BlockSpec((None, None, seq, LANES), lambda b, h, i: (b, h, 0, 0)),
                  pl.BlockSpec((None, None, n_q, VT_ROWS, tq), lambda b, h, i: (b, h, 0, 0, 0))],
        out_specs=pl.BlockSpec((None, None, tq, LANES), lambda b, h, i: (b, h, i, 0)),
        out_shape=jax.ShapeDtypeStruct((batch, DIFF_HEADS, seq, LANES), BF16),
        scratch_shapes=[pltpu.VMEM((LANES, 2 * tq), BF16),
                        pltpu.VMEM((tq, 2 * tq), F32),
                        pltpu.VMEM((tq, 2 * tq), F32),
                        pltpu.VMEM((1, 2 * tq), F32),
                        pltpu.VMEM((VT_ROWS, 2 * tq), F32)],
        compiler_params=_params(3),
        name="diff_attn",
    )(lq1, lk1, lq2, lk2, g, qt, k, vt)


def _post_kernel(mix_ref, qm_ref, kv_ref, w_ref, x_ref, g_ref, b_ref, o_ref, cat_ref):
    for c in range(MIX_SLABS):
        cat_ref[:, c * LANES:(c + 1) * LANES] = mix_ref[c]
    q = qm_ref[...].astype(BF16)
    for hd in range(MEM_HEADS):
        lo = hd * MEM_HEAD_DIM
        qh = q[:, lo:lo + MEM_HEAD_DIM]
        kh = kv_ref[:, lo:lo + MEM_HEAD_DIM]
        vh = kv_ref[:, MEM_WIDTH + lo:MEM_WIDTH + lo + MEM_HEAD_DIM]
        s = lax.dot_general(qh, kh, (((1,), (1,)), ((), ())), preferred_element_type=F32)
        s = s * (MEM_HEAD_DIM ** -0.5)
        p = jnp.exp(s - jnp.max(s, axis=-1, keepdims=True))
        denom = jnp.sum(p, axis=-1, keepdims=True)
        oh = jnp.dot(p.astype(BF16), vh, preferred_element_type=F32) / denom
        cat_ref[:, MIX_WIDTH + lo:MIX_WIDTH + lo + MEM_HEAD_DIM] = oh.astype(BF16)
    y = jnp.dot(cat_ref[...], w_ref[...], preferred_element_type=F32)
    o_ref[...] = _layernorm(DEEPNORM_ALPHA * x_ref[...] + y, g_ref[...], b_ref[...])


def _post(mix, qm_src, qm_block, kv, w_out, layer, x, g, b, seq, tm):
    t = x.shape[0]
    n_s = seq // tm
    return pl.pallas_call(
        _post_kernel,
        grid=(t // tm,),
        in_specs=[pl.BlockSpec((None, MIX_SLABS, tm, LANES), lambda i: (i // n_s, 0, i % n_s, 0)),
                  pl.BlockSpec((tm, MEM_WIDTH), lambda i: (i, qm_block)),
                  pl.BlockSpec((MEM_LEN, 2 * MEM_WIDTH), lambda i: (i // n_s, 0)),
                  pl.BlockSpec((None, D_MODEL, D_MODEL), lambda i: (layer, 0, 0),
                               pipeline_mode=pl.Buffered(1)),
                  pl.BlockSpec((tm, D_MODEL), lambda i: (i, 0)),
                  _const_spec(g.shape),
                  _const_spec(b.shape)],
        out_specs=pl.BlockSpec((tm, D_MODEL), lambda i: (i, 0)),
        out_shape=jax.ShapeDtypeStruct((t, D_MODEL), F32),
        scratch_shapes=[pltpu.VMEM((tm, D_MODEL), BF16)],
        compiler_params=_params(1),
        name="post_mix",
    )(mix, qm_src, kv, w_out, x, g, b)


def _ffn_kernel(x_ref, wg_ref, wu_ref, cwg_ref, cwu_ref, cbg_ref, cbu_ref, wd_ref, g_ref, b_ref, o_ref,
                xb_ref, carry_g_ref, carry_u_ref, *h_refs, tm, tf, tiles_per_seq):
    i = pl.program_id(0)
    f = pl.program_id(1)
    n_f = pl.num_programs(1)
    seq_start = (i % tiles_per_seq) == 0

    @pl.when(f == 0)
    def _():
        xb_ref[...] = x_ref[...].astype(BF16)
        o_ref[...] = jnp.zeros(o_ref.shape, F32)

    @pl.when((f == 0) & (i == 0))
    def _():
        carry_g_ref[...] = jnp.zeros(carry_g_ref.shape, F32)
        carry_u_ref[...] = jnp.zeros(carry_u_ref.shape, F32)

    def conv_branch(cols, w_ref, cw_ref, cb_ref, h_ref, carry_ref):
        h = jnp.dot(xb_ref[...], w_ref[:, cols], preferred_element_type=F32)
        prev = carry_ref[f, :, cols]
        h_ref[0:CONV_HALO, :] = jnp.where(seq_start, jnp.zeros_like(prev), prev)
        h_ref[CONV_HALO:CONV_HALO + tm, :] = h
        carry_ref[f, :, cols] = h[tm - CONV_HALO:tm, :]
        return (cb_ref[:, cols] + cw_ref[2:3, cols] * h
                + cw_ref[1:2, cols] * h_ref[CONV_HALO - 1:CONV_HALO - 1 + tm, :]
                + cw_ref[0:1, cols] * h_ref[CONV_HALO - 2:CONV_HALO - 2 + tm, :])

    total = None
    for c in range(tf // FFN_SUB):
        cols = slice(c * FFN_SUB, (c + 1) * FFN_SUB)
        gate = conv_branch(cols, wg_ref, cwg_ref, cbg_ref, h_refs[2 * c], carry_g_ref)
        up = conv_branch(cols, wu_ref, cwu_ref, cbu_ref, h_refs[2 * c + 1], carry_u_ref)
        act = gate * jax.nn.sigmoid(gate) * up
        part = jnp.dot(act.astype(BF16), wd_ref[cols, :], preferred_element_type=F32)
        total = part if total is None else total + part
    o_ref[...] += total

    @pl.when(f == n_f - 1)
    def _():
        o_ref[...] = _layernorm(DEEPNORM_ALPHA * x_ref[...] + o_ref[...], g_ref[...], b_ref[...])


def _ffn(x, w_up, conv_w, conv_b, w_down, layer, g, b, seq, tm, tf):
    t = x.shape[0]
    n_f = D_FF // tf
    return pl.pallas_call(
        functools.partial(_ffn_kernel, tm=tm, tf=tf, tiles_per_seq=seq // tm),
        grid=(t // tm, n_f),
        in_specs=[pl.BlockSpec((tm, D_MODEL), lambda i, f: (i, 0)),
                  pl.BlockSpec((None, D_MODEL, tf), lambda i, f: (layer, 0, f)),
                  pl.BlockSpec((None, D_MODEL, tf), lambda i, f: (layer, 0, n_f + f)),
                  pl.BlockSpec((CONV_WIDTH, tf), lambda i, f: (0, f)),
                  pl.BlockSpec((CONV_WIDTH, tf), lambda i, f: (0, n_f + f)),
                  pl.BlockSpec((1, tf), lambda i, f: (0, f)),
                  pl.BlockSpec((1, tf), lambda i, f: (0, n_f + f)),
                  pl.BlockSpec((None, tf, D_MODEL), lambda i, f: (layer, f, 0)),
                  _const_spec(g.shape),
                  _const_spec(b.shape)],
        out_specs=pl.BlockSpec((tm, D_MODEL), lambda i, f: (i, 0)),
        out_shape=jax.ShapeDtypeStruct((t, D_MODEL), F32),
        scratch_shapes=[pltpu.VMEM((tm, D_MODEL), BF16),
                        pltpu.VMEM((n_f, CONV_HALO, tf), F32),
                        pltpu.VMEM((n_f, CONV_HALO, tf), F32)]
                       + [pltpu.VMEM((tm + CONV_HALO, FFN_SUB), F32)] * (2 * (tf // FFN_SUB)),
        compiler_params=_params(2),
        name="conv_ffn",
    )(x, w_up, w_up, conv_w, conv_w, conv_b, conv_b, w_down, g, b)


def kernel(x, mem, positions, pool_w_in, pool_w, pool_scale, diff_w_in, diff_lambda_q1, diff_lambda_k1,
           diff_lambda_q2, diff_lambda_k2, diff_subln_g, mem_w_kv, w_out, ln1_g, ln1_b, ffn_w_up,
           ffn_conv_w, ffn_conv_b, ffn_w_down, ln2_g, ln2_b):
    batch, seq, _ = x.shape
    t = batch * seq
    row = lambda a: a.reshape(1, -1)

    xs = x.reshape(t, D_MODEL)
    mem2 = mem.reshape(batch * MEM_LEN, D_MODEL)
    pos = positions.reshape(t, 1)
    inv_freq = ROPE_THETA ** (-jnp.arange(0, ROPE_DIM, 2, dtype=F32) / ROPE_DIM)
    invf = jnp.tile(jnp.tile(inv_freq, DIFF_QK_DIM // ROPE_HALF), LANES // DIFF_QK_DIM).reshape(1, LANES)

    w_out_b = w_out.astype(BF16)
    w_up_b = ffn_w_up.astype(BF16)
    w_down_b = ffn_w_down.astype(BF16)

    for i in range(DEPTH):
        j = i // 2
        kv = _matmul(mem2, mem_w_kv[i].astype(BF16), BF16, tm=batch * MEM_LEN, tn=512)
        if i % 2 == 0:
            h = _matmul(xs, pool_w_in[j].astype(BF16), F32, tm=1024, tn=1024)
            mix = _pool(h, pool_w[j].astype(BF16), row(pool_scale[j]), batch, seq, tm=512)
            qm_src, qm_block = h, MIX_WIDTH // MEM_WIDTH
        else:
            qt, kk, vt, qm = _inproj_rope(xs, diff_w_in[j].astype(BF16), pos, invf, batch, seq,
                                          tm=1024, tn=MEM_WIDTH, tk=ATTN_TILE)
            lam_init = 0.8 - 0.6 * math.exp(-0.3 * i)
            mix = _diff_attn(qt, kk, vt, row(diff_lambda_q1[j]),
                             row(diff_lambda_k1[j]), row(diff_lambda_q2[j]), row(diff_lambda_k2[j]),
                             row(diff_subln_g[j]), batch, seq, tq=ATTN_TILE, lam_init=lam_init)
            qm_src, qm_block = qm, 0
        xs = _post(mix, qm_src, qm_block, kv, w_out_b, i, xs, row(ln1_g[i]), row(ln1_b[i]), seq, tm=512)
        xs = _ffn(xs, w_up_b, ffn_conv_w[i], row(ffn_conv_b[i]), w_down_b, i,
                  row(ln2_g[i]), row(ln2_b[i]), seq, tm=512, tf=512)
    return xs.reshape(batch, seq, D_MODEL)
```

```python
import functools
import math

import jax
import jax.numpy as jnp
from jax import lax
from jax.experimental import pallas as pl
from jax.experimental.pallas import tpu as pltpu

D_MODEL = 2048
DEPTH = 2
MEM_LEN = 256
MEM_HEADS = 4
MEM_HEAD_DIM = 128
MEM_WIDTH = MEM_HEADS * MEM_HEAD_DIM
MIX_WIDTH = D_MODEL - MEM_WIDTH
POOL_WINDOWS = (2, 4, 8, 16)
POOL_GROUP_DIM = MIX_WIDTH // len(POOL_WINDOWS)
POOL_HALO = 16
DIFF_HEADS = 12
DIFF_QK_DIM = 64
DIFF_V_DIM = 2 * DIFF_QK_DIM
VT_ROWS = DIFF_V_DIM + 16
ROPE_DIM = DIFF_QK_DIM // 4
ROPE_HALF = ROPE_DIM // 2
ROPE_THETA = 500000.0
D_FF = 5632
CONV_WIDTH = 3
CONV_HALO = 8
LN_EPS = 1e-5
RMS_EPS = 1e-6
DEEPNORM_ALPHA = (2.0 * DEPTH) ** 0.25
DIFF_IN_WIDTH = 3 * MIX_WIDTH + MEM_WIDTH
MASK_VALUE = -1e30
QK_SCALE_LOG2E = DIFF_QK_DIM ** -0.5 * math.log2(math.e)

LANES = 128
MXU_WIDTH = 256
MIX_SLABS = MIX_WIDTH // LANES
FFN_SUB = MXU_WIDTH
FFN_CHUNK = 512
ATTN_TILE = 512
VMEM_LIMIT = 56 * 1024 * 1024

BF16 = jnp.bfloat16
F32 = jnp.float32


def _params(n_axes):
    return pltpu.CompilerParams(dimension_semantics=("arbitrary",) * n_axes,
                                vmem_limit_bytes=VMEM_LIMIT)


def _const_spec(shape):
    return pl.BlockSpec(shape, lambda *_: (0,) * len(shape), pipeline_mode=pl.Buffered(1))


def _layernorm(z, g, b):
    mu = jnp.mean(z, axis=-1, keepdims=True)
    zc = z - mu
    var = jnp.mean(zc * zc, axis=-1, keepdims=True)
    return zc * lax.rsqrt(var + LN_EPS) * g + b


def _matmul_kernel(x_ref, w_ref, o_ref, xb_ref):
    @pl.when(pl.program_id(1) == 0)
    def _():
        xb_ref[...] = x_ref[...].astype(BF16)

    o_ref[...] = jnp.dot(xb_ref[...], w_ref[...], preferred_element_type=F32).astype(o_ref.dtype)


def _matmul(x, w, out_dtype, tm, tn):
    m, k = x.shape
    n = w.shape[1]
    return pl.pallas_call(
        _matmul_kernel,
        grid=(m // tm, n // tn),
        in_specs=[pl.BlockSpec((tm, k), lambda i, j: (i, 0)),
                  pl.BlockSpec((k, tn), lambda i, j: (0, j))],
        out_specs=pl.BlockSpec((tm, tn), lambda i, j: (i, j)),
        out_shape=jax.ShapeDtypeStruct((m, n), out_dtype),
        scratch_shapes=[pltpu.VMEM((tm, k), BF16)],
        compiler_params=_params(2),
        name="matmul",
    )(x, w)


def _inproj_rope_kernel(x_ref, w_ref, pos_ref, invf_ref, qt_ref, k_ref, vt_ref, qm_ref, xb_ref, tab_ref,
                        *, tm, tn, tk):
    j = pl.program_id(1)
    n_qk_tiles = 2 * MIX_WIDTH // tn
    n_q_tiles = MIX_WIDTH // tn

    @pl.when(j == 0)
    def _():
        xb_ref[...] = x_ref[...].astype(BF16)
        ang = pos_ref[...].astype(F32) * invf_ref[...]
        cos = jnp.cos(ang)
        sin = jnp.sin(ang)
        r = lax.broadcasted_iota(jnp.int32, ang.shape, 1) % DIFF_QK_DIM
        tab_ref[0] = jnp.where(r < ROPE_DIM, cos, 1.0)
        tab_ref[1] = jnp.where(r < ROPE_HALF, -sin, 0.0)
        tab_ref[2] = jnp.where((r >= ROPE_HALF) & (r < ROPE_DIM), sin, 0.0)

    def rope(a):
        up = pltpu.roll(a, LANES - ROPE_HALF, 1)
        dn = pltpu.roll(a, ROPE_HALF, 1)
        return a * tab_ref[0] + up * tab_ref[1] + dn * tab_ref[2]

    def head_slabs():
        per = MXU_WIDTH // LANES
        for piece in range(tn // MXU_WIDTH):
            acc = jnp.dot(xb_ref[...], w_ref[:, piece * MXU_WIDTH:(piece + 1) * MXU_WIDTH],
                          preferred_element_type=F32)
            for c in range(per):
                yield piece * per + c, acc[:, c * LANES:(c + 1) * LANES]

    @pl.when(j < n_q_tiles)
    def _():
        for c, a in head_slabs():
            qt = (rope(a) * QK_SCALE_LOG2E).T
            for s in range(tm // tk):
                qt_ref[c, s] = qt[:, s * tk:(s + 1) * tk].astype(BF16)

    @pl.when((j >= n_q_tiles) & (j < n_qk_tiles))
    def _():
        for c, a in head_slabs():
            k_ref[c] = rope(a).astype(BF16)

    @pl.when((j >= n_qk_tiles) & (j < n_qk_tiles + n_q_tiles))
    def _():
        for c, a in head_slabs():
            vt = a.T
            for s in range(tm // tk):
                vt_ref[c, s, 0:DIFF_V_DIM, :] = vt[:, s * tk:(s + 1) * tk].astype(BF16)
                vt_ref[c, s, DIFF_V_DIM:VT_ROWS, :] = jnp.ones((VT_ROWS - DIFF_V_DIM, tk), BF16)

    @pl.when(j >= n_qk_tiles + n_q_tiles)
    def _():
        for c, a in head_slabs():
            qm_ref[:, c * LANES:(c + 1) * LANES] = a.astype(BF16)


def _inproj_rope(x, w, pos, invf, batch, seq, tm, tn, tk):
    m, k = x.shape
    n = w.shape[1]
    assert tn == MEM_WIDTH and MIX_WIDTH % tn == 0
    n_s = seq // tm
    nq = MIX_WIDTH // tn
    clamp = lambda j, lo: jnp.clip(j - lo, 0, nq - 1)
    heads = tn // LANES
    return pl.pallas_call(
        functools.partial(_inproj_rope_kernel, tm=tm, tn=tn, tk=tk),
        grid=(m // tm, n // tn),
        in_specs=[pl.BlockSpec((tm, k), lambda i, j: (i, 0)),
                  pl.BlockSpec((k, tn), lambda i, j: (0, j)),
                  pl.BlockSpec((tm, 1), lambda i, j: (i, 0)),
                  pl.BlockSpec((1, LANES), lambda i, j: (0, 0))],
        out_specs=[pl.BlockSpec((None, heads, tm // tk, LANES, tk),
                                lambda i, j: (i // n_s, clamp(j, 0), i % n_s, 0, 0)),
                   pl.BlockSpec((None, heads, tm, LANES), lambda i, j: (i // n_s, clamp(j, nq), i % n_s, 0)),
                   pl.BlockSpec((None, heads, tm // tk, VT_ROWS, tk),
                                lambda i, j: (i // n_s, clamp(j, 2 * nq), i % n_s, 0, 0)),
                   pl.BlockSpec((tm, tn), lambda i, j: (i, 0))],
        out_shape=[jax.ShapeDtypeStruct((batch, DIFF_HEADS, seq // tk, LANES, tk), BF16),
                   jax.ShapeDtypeStruct((batch, DIFF_HEADS, seq, LANES), BF16),
                   jax.ShapeDtypeStruct((batch, DIFF_HEADS, seq // tk, VT_ROWS, tk), BF16),
                   jax.ShapeDtypeStruct((m, MEM_WIDTH), BF16)],
        scratch_shapes=[pltpu.VMEM((tm, k), BF16), pltpu.VMEM((3, tm, LANES), F32)],
        compiler_params=_params(2),
        name="inproj_rope",
    )(x, w, pos, invf)


def _pool_kernel(x_ref, win_ref, w_ref, scale_ref, o_ref, qm_ref, ext_ref, *, tm):
    i = pl.program_id(1)

    @pl.when(i == 0)
    def _():
        ext_ref[0:POOL_HALO, :] = jnp.zeros((POOL_HALO, MIX_WIDTH), F32)

    @pl.when(i > 0)
    def _():
        ext_ref[0:POOL_HALO, :] = ext_ref[tm:tm + POOL_HALO, :]

    xb = x_ref[...].astype(BF16)
    ext_ref[POOL_HALO:POOL_HALO + tm, :] = jnp.dot(xb, win_ref[:, 0:MIX_WIDTH], preferred_element_type=F32)
    qm_ref[...] = jnp.dot(xb, win_ref[:, MIX_WIDTH:D_MODEL], preferred_element_type=F32).astype(BF16)

    pos = i * tm + lax.broadcasted_iota(jnp.int32, (tm, 1), 0)
    for g, win in enumerate(POOL_WINDOWS):
        cols = slice(g * POOL_GROUP_DIM, (g + 1) * POOL_GROUP_DIM)
        cur = ext_ref[POOL_HALO:POOL_HALO + tm, cols]
        tot = cur
        for lag in range(1, win):
            tot = tot + ext_ref[POOL_HALO - lag:POOL_HALO - lag + tm, cols]
        cnt = jnp.minimum(pos + 1, win).astype(F32)
        d = tot / cnt - cur
        y = jnp.dot(d.astype(BF16), w_ref[g], preferred_element_type=F32) * scale_ref[:, cols]
        slabs = POOL_GROUP_DIM // LANES
        for c in range(slabs):
            o_ref[g * slabs + c] = y[:, c * LANES:(c + 1) * LANES].astype(o_ref.dtype)


def _pool(x, w_in, w_groups, scale, batch, seq, tm):
    n_s = seq // tm
    return pl.pallas_call(
        functools.partial(_pool_kernel, tm=tm),
        grid=(batch, n_s),
        in_specs=[pl.BlockSpec((tm, D_MODEL), lambda b, i: (b * n_s + i, 0)),
                  _const_spec(w_in.shape),
                  _const_spec(w_groups.shape),
                  _const_spec(scale.shape)],
        out_specs=[pl.BlockSpec((None, MIX_SLABS, tm, LANES), lambda b, i: (b, 0, i, 0)),
                   pl.BlockSpec((tm, MEM_WIDTH), lambda b, i: (b * n_s + i, 0))],
        out_shape=[jax.ShapeDtypeStruct((batch, MIX_SLABS, seq, LANES), BF16),
                   jax.ShapeDtypeStruct((batch * seq, MEM_WIDTH), BF16)],
        scratch_shapes=[pltpu.VMEM((tm + POOL_HALO, MIX_WIDTH), F32)],
        compiler_params=_params(2),
        name="pool_mixer",
    )(x, w_in, w_groups, scale)


def _diff_attn_kernel(lq1_ref, lk1_ref, lq2_ref, lk2_ref, g_ref, qt_ref, k_ref, vt_ref, o_ref,
                      w_ref, sa_ref, sb_ref, m_ref, acc_ref, *, tq, lam_init):
    i = pl.program_id(2)
    tk = tq

    qt = qt_ref[...]
    row = lax.broadcasted_iota(jnp.int32, qt.shape, 0)
    zero = jnp.zeros_like(qt)
    w_ref[:, 0:tq] = jnp.where(row < DIFF_QK_DIM, qt, zero)
    w_ref[:, tq:2 * tq] = jnp.where(row >= DIFF_QK_DIM, qt, zero)
    m_ref[...] = jnp.full(m_ref.shape, MASK_VALUE, F32)
    acc_ref[...] = jnp.zeros(acc_ref.shape, F32)

    def scores(j, s_ref):
        start = pl.multiple_of(j * tk, tk)
        s_ref[...] = jnp.dot(k_ref[pl.ds(start, tk), :], w_ref[...], preferred_element_type=F32)

    def softmax_pv(j, s_ref, masked):
        s = s_ref[...]
        if masked:
            kpos = lax.broadcasted_iota(jnp.int32, s.shape, 0)
            qpos = lax.broadcasted_iota(jnp.int32, s.shape, 1) % tq
            s = jnp.where(kpos <= qpos, s, MASK_VALUE)
        m_old = m_ref[...]
        m_new = jnp.maximum(m_old, jnp.max(s, axis=0, keepdims=True))
        alpha = jnp.exp2(m_old - m_new)
        p = jnp.exp2(s - m_new).astype(BF16)
        acc_ref[...] = alpha * acc_ref[...] + jnp.dot(vt_ref[j], p, preferred_element_type=F32)
        m_ref[...] = m_new

    scores(0, sa_ref)

    def pair(jj, carry):
        j = 2 * jj
        scores(j + 1, sb_ref)
        softmax_pv(j, sa_ref, masked=False)
        scores(j + 2, sa_ref)
        softmax_pv(j + 1, sb_ref, masked=False)
        return carry

    lax.fori_loop(0, i // 2, pair, 0)

    @pl.when(i % 2 == 0)
    def _():
        softmax_pv(i, sa_ref, masked=True)

    @pl.when(i % 2 == 1)
    def _():
        scores(i, sb_ref)
        softmax_pv(i - 1, sa_ref, masked=False)
        softmax_pv(i, sb_ref, masked=True)

    lam = (jnp.exp(jnp.sum(lq1_ref[...] * lk1_ref[...], axis=-1, keepdims=True))
           - jnp.exp(jnp.sum(lq2_ref[...] * lk2_ref[...], axis=-1, keepdims=True)) + lam_init)
    norm = acc_ref[DIFF_V_DIM:DIFF_V_DIM + 1, :]
    o1 = acc_ref[0:DIFF_V_DIM, 0:tq] / norm[:, 0:tq]
    o2 = acc_ref[0:DIFF_V_DIM, tq:2 * tq] / norm[:, tq:2 * tq]
    ot = o1 - lam * o2
    ot = ot * lax.rsqrt(jnp.mean(ot * ot, axis=0, keepdims=True) + RMS_EPS)
    o_ref[...] = (ot.T * g_ref[...] * (1.0 - lam_init)).astype(o_ref.dtype)


def _diff_attn(qt, k, vt, lq1, lk1, lq2, lk2, g, batch, seq, tq, lam_init):
    n_q = seq // tq
    vec = lambda n: pl.BlockSpec((1, n), lambda b, h, i: (0, 0))
    return pl.pallas_call(
        functools.partial(_diff_attn_kernel, tq=tq, lam_init=lam_init),
        grid=(batch, DIFF_HEADS, n_q),
        in_specs=[vec(DIFF_QK_DIM), vec(DIFF_QK_DIM), vec(DIFF_QK_DIM), vec(DIFF_QK_DIM), vec(DIFF_V_DIM),
                  pl.BlockSpec((None, None, None, LANES, tq), lambda b, h, i: (b, h, i, 0, 0)),
                  pl.BlockSpec((None, None, seq, LANES), lambda b, h, i: (b, h, 0, 0)),
                  pl.BlockSpec((None, None, n_q, VT_ROWS, tq), lambda b, h, i: (b, h, 0, 0, 0))],
        out_specs=pl.BlockSpec((None, None, tq, LANES), lambda b, h, i: (b, h, i, 0)),
        out_shape=jax.ShapeDtypeStruct((batch, DIFF_HEADS, seq, LANES), BF16),
        scratch_shapes=[pltpu.VMEM((LANES, 2 * tq), BF16),
                        pltpu.VMEM((tq, 2 * tq), F32),
                        pltpu.VMEM((tq, 2 * tq), F32),
                        pltpu.VMEM((1, 2 * tq), F32),
                        pltpu.VMEM((VT_ROWS, 2 * tq), F32)],
        compiler_params=_params(3),
        name="diff_attn",
    )(lq1, lk1, lq2, lk2, g, qt, k, vt)


def _post_kernel(mix_ref, qm_ref, kv_ref, w_ref, x_ref, g_ref, b_ref, o_ref, cat_ref):
    for c in range(MIX_SLABS):
        cat_ref[:, c * LANES:(c + 1) * LANES] = mix_ref[c]
    q = qm_ref[...].astype(BF16)
    for hd in range(MEM_HEADS):
        lo = hd * MEM_HEAD_DIM
        qh = q[:, lo:lo + MEM_HEAD_DIM]
        kh = kv_ref[:, lo:lo + MEM_HEAD_DIM]
        vh = kv_ref[:, MEM_WIDTH + lo:MEM_WIDTH + lo + MEM_HEAD_DIM]
        s = lax.dot_general(qh, kh, (((1,), (1,)), ((), ())), preferred_element_type=F32)
        s = s * (MEM_HEAD_DIM ** -0.5)
        p = jnp.exp(s - jnp.max(s, axis=-1, keepdims=True))
        denom = jnp.sum(p, axis=-1, keepdims=True)
        oh = jnp.dot(p.astype(BF16), vh, preferred_element_type=F32) / denom
        cat_ref[:, MIX_WIDTH + lo:MIX_WIDTH + lo + MEM_HEAD_DIM] = oh.astype(BF16)
    y = jnp.dot(cat_ref[...], w_ref[...], preferred_element_type=F32)
    o_ref[...] = _layernorm(DEEPNORM_ALPHA * x_ref[...] + y, g_ref[...], b_ref[...])


def _post(mix, qm, kv, w_out, layer, x, g, b, seq, tm):
    t = x.shape[0]
    n_s = seq // tm
    return pl.pallas_call(
        _post_kernel,
        grid=(t // tm,),
        in_specs=[pl.BlockSpec((None, MIX_SLABS, tm, LANES), lambda i: (i // n_s, 0, i % n_s, 0)),
                  pl.BlockSpec((tm, MEM_WIDTH), lambda i: (i, 0)),
                  pl.BlockSpec((MEM_LEN, 2 * MEM_WIDTH), lambda i: (i // n_s, 0)),
                  pl.BlockSpec((None, D_MODEL, D_MODEL), lambda i: (layer, 0, 0),
                               pipeline_mode=pl.Buffered(1)),
                  pl.BlockSpec((tm, D_MODEL), lambda i: (i, 0)),
                  _const_spec(g.shape),
                  _const_spec(b.shape)],
        out_specs=pl.BlockSpec((tm, D_MODEL), lambda i: (i, 0)),
        out_shape=jax.ShapeDtypeStruct((t, D_MODEL), F32),
        scratch_shapes=[pltpu.VMEM((tm, D_MODEL), BF16)],
        compiler_params=_params(1),
        name="post_mix",
    )(mix, qm, kv, w_out, x, g, b)


def _ffn_kernel(x_ref, w_ref, cwb_ref, wd_ref, g_ref, b_ref, o_ref,
                xb_ref, carry_ref, *h_refs, tm, tf, tiles_per_seq):
    i = pl.program_id(0)
    f = pl.program_id(1)
    n_f = pl.num_programs(1)
    seq_start = (i % tiles_per_seq) == 0

    @pl.when(f == 0)
    def _():
        x = x_ref[...]
        xb_ref[...] = x.astype(BF16)
        o_ref[...] = DEEPNORM_ALPHA * x

    @pl.when((f == 0) & (i == 0))
    def _():
        carry_ref[...] = jnp.zeros(carry_ref.shape, F32)

    def conv_branch(branch, cols, h_ref):
        wcols = slice(branch * tf + cols.start, branch * tf + cols.stop)
        h = jnp.dot(xb_ref[...], w_ref[:, wcols], preferred_element_type=F32)
        prev = carry_ref[f, branch, :, cols]
        h_ref[0:CONV_HALO, :] = jnp.where(seq_start, jnp.zeros_like(prev), prev)
        h_ref[CONV_HALO:CONV_HALO + tm, :] = h
        carry_ref[f, branch, :, cols] = h[tm - CONV_HALO:tm, :]
        tap = lambda r: cwb_ref[f, branch, r:r + 1, cols]
        return (tap(CONV_WIDTH) + tap(2) * h
                + tap(1) * h_ref[CONV_HALO - 1:CONV_HALO - 1 + tm, :]
                + tap(0) * h_ref[CONV_HALO - 2:CONV_HALO - 2 + tm, :])

    total = None
    for c in range(tf // FFN_SUB):
        cols = slice(c * FFN_SUB, (c + 1) * FFN_SUB)
        gate = conv_branch(0, cols, h_refs[2 * c])
        up = conv_branch(1, cols, h_refs[2 * c + 1])
        act = gate * jax.nn.sigmoid(gate) * up
        part = jnp.dot(act.astype(BF16), wd_ref[cols, :], preferred_element_type=F32)
        total = part if total is None else total + part
    o_ref[...] += total

    @pl.when(f == n_f - 1)
    def _():
        o_ref[...] = _layernorm(o_ref[...], g_ref[...], b_ref[...])


def _ffn_operands(w_up, conv_w, conv_b, tf):
    n_l = w_up.shape[0]
    n_f = D_FF // tf
    w = w_up.reshape(n_l, D_MODEL, 2, n_f, tf).transpose(0, 3, 1, 2, 4)
    w = w.reshape(n_l, n_f, D_MODEL, 2 * tf).astype(BF16)
    cwb = jnp.concatenate([conv_w, conv_b[:, None, :]], axis=1)
    cwb = cwb.reshape(n_l, CONV_WIDTH + 1, 2, n_f, tf).transpose(0, 3, 2, 1, 4)
    return w, cwb


def _ffn(x, w_up, cwb, w_down, layer, g, b, seq, tm, tf):
    t = x.shape[0]
    n_f = D_FF // tf
    return pl.pallas_call(
        functools.partial(_ffn_kernel, tm=tm, tf=tf, tiles_per_seq=seq // tm),
        grid=(t // tm, n_f),
        in_specs=[pl.BlockSpec((tm, D_MODEL), lambda i, f: (i, 0)),
                  pl.BlockSpec((None, None, D_MODEL, 2 * tf), lambda i, f: (layer, f, 0, 0)),
                  pl.BlockSpec((None, n_f, 2, CONV_WIDTH + 1, tf), lambda i, f: (layer, 0, 0, 0, 0),
                               pipeline_mode=pl.Buffered(1)),
                  pl.BlockSpec((None, tf, D_MODEL), lambda i, f: (layer, f, 0)),
                  _const_spec(g.shape),
                  _const_spec(b.shape)],
        out_specs=pl.BlockSpec((tm, D_MODEL), lambda i, f: (i, 0)),
        out_shape=jax.ShapeDtypeStruct((t, D_MODEL), F32),
        scratch_shapes=[pltpu.VMEM((tm, D_MODEL), BF16),
                        pltpu.VMEM((n_f, 2, CONV_HALO, tf), F32)]
                       + [pltpu.VMEM((tm + CONV_HALO, FFN_SUB), F32)] * (2 * (tf // FFN_SUB)),
        compiler_params=_params(2),
        name="conv_ffn",
    )(x, w_up, cwb, w_down, g, b)


def kernel(x, mem, positions, pool_w_in, pool_w, pool_scale, diff_w_in, diff_lambda_q1, diff_lambda_k1,
           diff_lambda_q2, diff_lambda_k2, diff_subln_g, mem_w_kv, w_out, ln1_g, ln1_b, ffn_w_up,
           ffn_conv_w, ffn_conv_b, ffn_w_down, ln2_g, ln2_b):
    batch, seq, _ = x.shape
    t = batch * seq
    row = lambda a: a.reshape(1, -1)

    xs = x.reshape(t, D_MODEL)
    mem2 = mem.reshape(batch * MEM_LEN, D_MODEL)
    pos = positions.reshape(t, 1)
    inv_freq = ROPE_THETA ** (-jnp.arange(0, ROPE_DIM, 2, dtype=F32) / ROPE_DIM)
    invf = jnp.tile(jnp.tile(inv_freq, DIFF_QK_DIM // ROPE_HALF), LANES // DIFF_QK_DIM).reshape(1, LANES)

    w_out_b = w_out.astype(BF16)
    w_up_b, cwb = _ffn_operands(ffn_w_up, ffn_conv_w, ffn_conv_b, FFN_CHUNK)
    w_down_b = ffn_w_down.astype(BF16)

    for i in range(DEPTH):
        j = i // 2
        kv = _matmul(mem2, mem_w_kv[i].astype(BF16), BF16, tm=batch * MEM_LEN, tn=512)
        if i % 2 == 0:
            mix, qm = _pool(xs, pool_w_in[j].astype(BF16), pool_w[j].astype(BF16), row(pool_scale[j]),
                            batch, seq, tm=512)
        else:
            qt, kk, vt, qm = _inproj_rope(xs, diff_w_in[j].astype(BF16), pos, invf, batch, seq,
                                          tm=1024, tn=MEM_WIDTH, tk=ATTN_TILE)
            lam_init = 0.8 - 0.6 * math.exp(-0.3 * i)
            mix = _diff_attn(qt, kk, vt, row(diff_lambda_q1[j]),
                             row(diff_lambda_k1[j]), row(diff_lambda_q2[j]), row(diff_lambda_k2[j]),
                             row(diff_subln_g[j]), batch, seq, tq=ATTN_TILE, lam_init=lam_init)
        xs = _post(mix, qm, kv, w_out_b, i, xs, row(ln1_g[i]), row(ln1_b[i]), seq, tm=512)
        xs = _ffn(xs, w_up_b, cwb, w_down_b, i, row(ln2_g[i]), row(ln2_b[i]), seq, tm=512, tf=FFN_CHUNK)
    return xs.reshape(batch, seq, D_MODEL)
```

```python
import functools
import math

import jax
import jax.numpy as jnp
from jax import lax
from jax.experimental import pallas as pl
from jax.experimental.pallas import tpu as pltpu

D_MODEL = 2048
DEPTH = 2
MEM_LEN = 256
MEM_HEADS = 4
MEM_HEAD_DIM = 128
MEM_WIDTH = MEM_HEADS * MEM_HEAD_DIM
MIX_WIDTH = D_MODEL - MEM_WIDTH
POOL_WINDOWS = (2, 4, 8, 16)
POOL_GROUP_DIM = MIX_WIDTH // len(POOL_WINDOWS)
POOL_HALO = 16
DIFF_HEADS = 12
DIFF_QK_DIM = 64
DIFF_V_DIM = 2 * DIFF_QK_DIM
VT_ROWS = DIFF_V_DIM + 16
ROPE_DIM = DIFF_QK_DIM // 4
ROPE_HALF = ROPE_DIM // 2
ROPE_THETA = 500000.0
D_FF = 5632
CONV_WIDTH = 3
CONV_HALO = 8
LN_EPS = 1e-5
RMS_EPS = 1e-6
DEEPNORM_ALPHA = (2.0 * DEPTH) ** 0.25
DIFF_IN_WIDTH = 3 * MIX_WIDTH + MEM_WIDTH
MASK_VALUE = -1e30
QK_SCALE_LOG2E = DIFF_QK_DIM ** -0.5 * math.log2(math.e)

LANES = 128
MXU_WIDTH = 256
MIX_SLABS = MIX_WIDTH // LANES
FFN_SUB = MXU_WIDTH
FFN_CHUNK = 512
ATTN_TILE = 512
VMEM_LIMIT = 56 * 1024 * 1024

BF16 = jnp.bfloat16
F32 = jnp.float32


def _params(n_axes):
    return pltpu.CompilerParams(dimension_semantics=("arbitrary",) * n_axes,
                                vmem_limit_bytes=VMEM_LIMIT)


def _const_spec(shape):
    return pl.BlockSpec(shape, lambda *_: (0,) * len(shape), pipeline_mode=pl.Buffered(1))


def _layernorm(z, g, b):
    mu = jnp.mean(z, axis=-1, keepdims=True)
    zc = z - mu
    var = jnp.mean(zc * zc, axis=-1, keepdims=True)
    return zc * lax.rsqrt(var + LN_EPS) * g + b


def _matmul_kernel(x_ref, w_ref, o_ref, xb_ref):
    @pl.when(pl.program_id(1) == 0)
    def _():
        xb_ref[...] = x_ref[...].astype(BF16)

    o_ref[...] = jnp.dot(xb_ref[...], w_ref[...], preferred_element_type=F32).astype(o_ref.dtype)


def _matmul(x, w, out_dtype, tm, tn):
    m, k = x.shape
    n = w.shape[1]
    return pl.pallas_call(
        _matmul_kernel,
        grid=(m // tm, n // tn),
        in_specs=[pl.BlockSpec((tm, k), lambda i, j: (i, 0)),
                  pl.BlockSpec((k, tn), lambda i, j: (0, j))],
        out_specs=pl.BlockSpec((tm, tn), lambda i, j: (i, j)),
        out_shape=jax.ShapeDtypeStruct((m, n), out_dtype),
        scratch_shapes=[pltpu.VMEM((tm, k), BF16)],
        compiler_params=_params(2),
        name="matmul",
    )(x, w)


def _inproj_rope_kernel(x_ref, w_ref, pos_ref, invf_ref, qt_ref, k_ref, vt_ref, qm_ref, xb_ref, tab_ref):
    xb_ref[...] = x_ref[...].astype(BF16)
    ang = pos_ref[...].astype(F32) * invf_ref[...]
    cos = jnp.cos(ang)
    sin = jnp.sin(ang)
    r = lax.broadcasted_iota(jnp.int32, ang.shape, 1) % DIFF_QK_DIM
    tab_ref[0] = jnp.where(r < ROPE_DIM, cos, 1.0)
    tab_ref[1] = jnp.where(r < ROPE_HALF, -sin, 0.0)
    tab_ref[2] = jnp.where((r >= ROPE_HALF) & (r < ROPE_DIM), sin, 0.0)

    def rope(a):
        up = pltpu.roll(a, LANES - ROPE_HALF, 1)
        dn = pltpu.roll(a, ROPE_HALF, 1)
        return a * tab_ref[0] + up * tab_ref[1] + dn * tab_ref[2]

    per = MXU_WIDTH // LANES
    for piece in range(DIFF_IN_WIDTH // MXU_WIDTH):
        acc = jnp.dot(xb_ref[...], w_ref[:, piece * MXU_WIDTH:(piece + 1) * MXU_WIDTH],
                      preferred_element_type=F32)
        for c in range(per):
            a = acc[:, c * LANES:(c + 1) * LANES]
            kind, idx = divmod(piece * per + c, DIFF_HEADS)
            if kind == 0:
                qt_ref[idx] = (rope(a) * QK_SCALE_LOG2E).T.astype(BF16)
            elif kind == 1:
                k_ref[idx] = rope(a).astype(BF16)
            elif kind == 2:
                vt_ref[idx, 0:DIFF_V_DIM, :] = a.T.astype(BF16)
                vt_ref[idx, DIFF_V_DIM:VT_ROWS, :] = jnp.ones((VT_ROWS - DIFF_V_DIM, a.shape[0]), BF16)
            else:
                qm_ref[:, idx * LANES:(idx + 1) * LANES] = a.astype(BF16)


def _inproj_rope(x, w, pos, invf, batch, seq, tm):
    m, k = x.shape
    n_s = seq // tm
    return pl.pallas_call(
        _inproj_rope_kernel,
        grid=(m // tm,),
        in_specs=[pl.BlockSpec((tm, k), lambda i: (i, 0)),
                  _const_spec(w.shape),
                  pl.BlockSpec((tm, 1), lambda i: (i, 0)),
                  _const_spec(invf.shape)],
        out_specs=[pl.BlockSpec((None, DIFF_HEADS, None, LANES, tm), lambda i: (i // n_s, 0, i % n_s, 0, 0)),
                   pl.BlockSpec((None, DIFF_HEADS, tm, LANES), lambda i: (i // n_s, 0, i % n_s, 0)),
                   pl.BlockSpec((None, DIFF_HEADS, None, VT_ROWS, tm), lambda i: (i // n_s, 0, i % n_s, 0, 0)),
                   pl.BlockSpec((tm, MEM_WIDTH), lambda i: (i, 0))],
        out_shape=[jax.ShapeDtypeStruct((batch, DIFF_HEADS, n_s, LANES, tm), BF16),
                   jax.ShapeDtypeStruct((batch, DIFF_HEADS, seq, LANES), BF16),
                   jax.ShapeDtypeStruct((batch, DIFF_HEADS, n_s, VT_ROWS, tm), BF16),
                   jax.ShapeDtypeStruct((m, MEM_WIDTH), BF16)],
        scratch_shapes=[pltpu.VMEM((tm, k), BF16), pltpu.VMEM((3, tm, LANES), F32)],
        compiler_params=_params(1),
        name="inproj_rope",
    )(x, w, pos, invf)


def _pool_kernel(x_ref, win_ref, w_ref, scale_ref, o_ref, qm_ref, ext_ref, *, tm):
    i = pl.program_id(1)

    @pl.when(i == 0)
    def _():
        ext_ref[0:POOL_HALO, :] = jnp.zeros((POOL_HALO, MIX_WIDTH), F32)

    @pl.when(i > 0)
    def _():
        ext_ref[0:POOL_HALO, :] = ext_ref[tm:tm + POOL_HALO, :]

    xb = x_ref[...].astype(BF16)
    ext_ref[POOL_HALO:POOL_HALO + tm, :] = jnp.dot(xb, win_ref[:, 0:MIX_WIDTH], preferred_element_type=F32)
    qm_ref[...] = jnp.dot(xb, win_ref[:, MIX_WIDTH:D_MODEL], preferred_element_type=F32).astype(BF16)

    pos = i * tm + lax.broadcasted_iota(jnp.int32, (tm, 1), 0)
    for g, win in enumerate(POOL_WINDOWS):
        cols = slice(g * POOL_GROUP_DIM, (g + 1) * POOL_GROUP_DIM)
        cur = ext_ref[POOL_HALO:POOL_HALO + tm, cols]
        tot = cur
        for lag in range(1, win):
            tot = tot + ext_ref[POOL_HALO - lag:POOL_HALO - lag + tm, cols]
        cnt = jnp.minimum(pos + 1, win).astype(F32)
        d = tot / cnt - cur
        y = jnp.dot(d.astype(BF16), w_ref[g], preferred_element_type=F32) * scale_ref[:, cols]
        slabs = POOL_GROUP_DIM // LANES
        for c in range(slabs):
            o_ref[g * slabs + c] = y[:, c * LANES:(c + 1) * LANES].astype(o_ref.dtype)


def _pool(x, w_in, w_groups, scale, batch, seq, tm):
    n_s = seq // tm
    return pl.pallas_call(
        functools.partial(_pool_kernel, tm=tm),
        grid=(batch, n_s),
        in_specs=[pl.BlockSpec((tm, D_MODEL), lambda b, i: (b * n_s + i, 0)),
                  _const_spec(w_in.shape),
                  _const_spec(w_groups.shape),
                  _const_spec(scale.shape)],
        out_specs=[pl.BlockSpec((None, MIX_SLABS, tm, LANES), lambda b, i: (b, 0, i, 0)),
                   pl.BlockSpec((tm, MEM_WIDTH), lambda b, i: (b * n_s + i, 0))],
        out_shape=[jax.ShapeDtypeStruct((batch, MIX_SLABS, seq, LANES), BF16),
                   jax.ShapeDtypeStruct((batch * seq, MEM_WIDTH), BF16)],
        scratch_shapes=[pltpu.VMEM((tm + POOL_HALO, MIX_WIDTH), F32)],
        compiler_params=_params(2),
        name="pool_mixer",
    )(x, w_in, w_groups, scale)


def _diff_attn_kernel(lq1_ref, lk1_ref, lq2_ref, lk2_ref, g_ref, qt_ref, k_ref, vt_ref, o_ref,
                      w_ref, sa_ref, sb_ref, m_ref, acc_ref, *, tq, lam_init):
    i = pl.program_id(2)
    tk = tq

    qt = qt_ref[...]
    row = lax.broadcasted_iota(jnp.int32, qt.shape, 0)
    zero = jnp.zeros_like(qt)
    w_ref[:, 0:tq] = jnp.where(row < DIFF_QK_DIM, qt, zero)
    w_ref[:, tq:2 * tq] = jnp.where(row >= DIFF_QK_DIM, qt, zero)
    m_ref[...] = jnp.full(m_ref.shape, MASK_VALUE, F32)
    acc_ref[...] = jnp.zeros(acc_ref.shape, F32)

    def scores(j, s_ref):
        start = pl.multiple_of(j * tk, tk)
        s_ref[...] = jnp.dot(k_ref[pl.ds(start, tk), :], w_ref[...], preferred_element_type=F32)

    def softmax_pv(j, s_ref, masked):
        s = s_ref[...]
        if masked:
            kpos = lax.broadcasted_iota(jnp.int32, s.shape, 0)
            qpos = lax.broadcasted_iota(jnp.int32, s.shape, 1) % tq
            s = jnp.where(kpos <= qpos, s, MASK_VALUE)
        m_old = m_ref[...]
        m_new = jnp.maximum(m_old, jnp.max(s, axis=0, keepdims=True))
        alpha = jnp.exp2(m_old - m_new)
        p = jnp.exp2(s - m_new).astype(BF16)
        acc_ref[...] = alpha * acc_ref[...] + jnp.dot(vt_ref[j], p, preferred_element_type=F32)
        m_ref[...] = m_new

    scores(0, sa_ref)

    def pair(jj, carry):
        j = 2 * jj
        scores(j + 1, sb_ref)
        softmax_pv(j, sa_ref, masked=False)
        scores(j + 2, sa_ref)
        softmax_pv(j + 1, sb_ref, masked=False)
        return carry

    lax.fori_loop(0, i // 2, pair, 0)

    @pl.when(i % 2 == 0)
    def _():
        softmax_pv(i, sa_ref, masked=True)

    @pl.when(i % 2 == 1)
    def _():
        scores(i, sb_ref)
        softmax_pv(i - 1, sa_ref, masked=False)
        softmax_pv(i, sb_ref, masked=True)

    lam = (jnp.exp(jnp.sum(lq1_ref[...] * lk1_ref[...], axis=-1, keepdims=True))
           - jnp.exp(jnp.sum(lq2_ref[...] * lk2_ref[...], axis=-1, keepdims=True)) + lam_init)
    norm = acc_ref[DIFF_V_DIM:DIFF_V_DIM + 1, :]
    o1 = acc_ref[0:DIFF_V_DIM, 0:tq] / norm[:, 0:tq]
    o2 = acc_ref[0:DIFF_V_DIM, tq:2 * tq] / norm[:, tq:2 * tq]
    ot = o1 - lam * o2
    ot = ot * lax.rsqrt(jnp.mean(ot * ot, axis=0, keepdims=True) + RMS_EPS)
    o_ref[...] = (ot.T * g_ref[...] * (1.0 - lam_init)).astype(o_ref.dtype)


def _diff_attn(qt, k, vt, lq1, lk1, lq2, lk2, g, batch, seq, tq, lam_init):
    n_q = seq // tq
    vec = lambda n: pl.BlockSpec((1, n), lambda b, h, i: (0, 0))
    return pl.pallas_call(
        functools.partial(_diff_attn_kernel, tq=tq, lam_init=lam_init),
        grid=(batch, DIFF_HEADS, n_q),
        in_specs=[vec(DIFF_QK_DIM), vec(DIFF_QK_DIM), vec(DIFF_QK_DIM), vec(DIFF_QK_DIM), vec(DIFF_V_DIM),
                  pl.BlockSpec((None, None, None, LANES, tq), lambda b, h, i: (b, h, i, 0, 0)),
                  pl.BlockSpec((None, None, seq, LANES), lambda b, h, i: (b, h, 0, 0)),
                  pl.BlockSpec((None, None, n_q, VT_ROWS, tq), lambda b, h, i: (b, h, 0, 0, 0))],
        out_specs=pl.BlockSpec((None, None, tq, LANES), lambda b, h, i: (b, h, i, 0)),
        out_shape=jax.ShapeDtypeStruct((batch, DIFF_HEADS, seq, LANES), BF16),
        scratch_shapes=[pltpu.VMEM((LANES, 2 * tq), BF16),
                        pltpu.VMEM((tq, 2 * tq), F32),
                        pltpu.VMEM((tq, 2 * tq), F32),
                        pltpu.VMEM((1, 2 * tq), F32),
                        pltpu.VMEM((VT_ROWS, 2 * tq), F32)],
        compiler_params=_params(3),
        name="diff_attn",
    )(lq1, lk1, lq2, lk2, g, qt, k, vt)


def _post_kernel(mix_ref, qm_ref, kv_ref, w_ref, x_ref, g_ref, b_ref, o_ref, cat_ref):
    for c in range(MIX_SLABS):
        cat_ref[:, c * LANES:(c + 1) * LANES] = mix_ref[c]
    q = qm_ref[...].astype(BF16)
    for hd in range(MEM_HEADS):
        lo = hd * MEM_HEAD_DIM
        qh = q[:, lo:lo + MEM_HEAD_DIM]
        kh = kv_ref[:, lo:lo + MEM_HEAD_DIM]
        vh = kv_ref[:, MEM_WIDTH + lo:MEM_WIDTH + lo + MEM_HEAD_DIM]
        s = lax.dot_general(qh, kh, (((1,), (1,)), ((), ())), preferred_element_type=F32)
        s = s * (MEM_HEAD_DIM ** -0.5)
        p = jnp.exp(s - jnp.max(s, axis=-1, keepdims=True))
        denom = jnp.sum(p, axis=-1, keepdims=True)
        oh = jnp.dot(p.astype(BF16), vh, preferred_element_type=F32) / denom
        cat_ref[:, MIX_WIDTH + lo:MIX_WIDTH + lo + MEM_HEAD_DIM] = oh.astype(BF16)
    y = jnp.dot(cat_ref[...], w_ref[...], preferred_element_type=F32)
    o_ref[...] = _layernorm(DEEPNORM_ALPHA * x_ref[...] + y, g_ref[...], b_ref[...])


def _post(mix, qm, kv, w_out, layer, x, g, b, seq, tm):
    t = x.shape[0]
    n_s = seq // tm
    return pl.pallas_call(
        _post_kernel,
        grid=(t // tm,),
        in_specs=[pl.BlockSpec((None, MIX_SLABS, tm, LANES), lambda i: (i // n_s, 0, i % n_s, 0)),
                  pl.BlockSpec((tm, MEM_WIDTH), lambda i: (i, 0)),
                  pl.BlockSpec((MEM_LEN, 2 * MEM_WIDTH), lambda i: (i // n_s, 0)),
                  pl.BlockSpec((None, D_MODEL, D_MODEL), lambda i: (layer, 0, 0),
                               pipeline_mode=pl.Buffered(1)),
                  pl.BlockSpec((tm, D_MODEL), lambda i: (i, 0)),
                  _const_spec(g.shape),
                  _const_spec(b.shape)],
        out_specs=pl.BlockSpec((tm, D_MODEL), lambda i: (i, 0)),
        out_shape=jax.ShapeDtypeStruct((t, D_MODEL), F32),
        scratch_shapes=[pltpu.VMEM((tm, D_MODEL), BF16)],
        compiler_params=_params(1),
        name="post_mix",
    )(mix, qm, kv, w_out, x, g, b)


def _ffn_kernel(x_ref, wg_ref, wu_ref, cwb_ref, wd_ref, g_ref, b_ref, o_ref,
                xb_ref, carry_ref, *h_refs, tm, tf, tiles_per_seq):
    i = pl.program_id(0)
    f = pl.program_id(1)
    n_f = pl.num_programs(1)
    seq_start = (i % tiles_per_seq) == 0

    @pl.when(f == 0)
    def _():
        x = x_ref[...]
        xb_ref[...] = x.astype(BF16)
        o_ref[...] = DEEPNORM_ALPHA * x

    @pl.when((f == 0) & (i == 0))
    def _():
        carry_ref[...] = jnp.zeros(carry_ref.shape, F32)

    def conv_branch(branch, cols, h_ref):
        w_ref = (wg_ref, wu_ref)[branch]
        h = jnp.dot(xb_ref[...], w_ref[:, cols], preferred_element_type=F32)
        prev = carry_ref[f, branch, :, cols]
        h_ref[0:CONV_HALO, :] = jnp.where(seq_start, jnp.zeros_like(prev), prev)
        h_ref[CONV_HALO:CONV_HALO + tm, :] = h
        carry_ref[f, branch, :, cols] = h[tm - CONV_HALO:tm, :]
        tap = lambda r: cwb_ref[f, branch, r:r + 1, cols]
        return (tap(CONV_WIDTH) + tap(2) * h
                + tap(1) * h_ref[CONV_HALO - 1:CONV_HALO - 1 + tm, :]
                + tap(0) * h_ref[CONV_HALO - 2:CONV_HALO - 2 + tm, :])

    total = None
    for c in range(tf // FFN_SUB):
        cols = slice(c * FFN_SUB, (c + 1) * FFN_SUB)
        gate = conv_branch(0, cols, h_refs[2 * c])
        up = conv_branch(1, cols, h_refs[2 * c + 1])
        act = gate * jax.nn.sigmoid(gate) * up
        part = jnp.dot(act.astype(BF16), wd_ref[cols, :], preferred_element_type=F32)
        total = part if total is None else total + part
    o_ref[...] += total

    @pl.when(f == n_f - 1)
    def _():
        o_ref[...] = _layernorm(o_ref[...], g_ref[...], b_ref[...])


def _conv_operands(conv_w, conv_b, tf):
    n_l = conv_w.shape[0]
    n_f = D_FF // tf
    cwb = jnp.concatenate([conv_w, conv_b[:, None, :]], axis=1)
    return cwb.reshape(n_l, CONV_WIDTH + 1, 2, n_f, tf).transpose(0, 3, 2, 1, 4)


def _ffn(x, w_up, cwb, w_down, layer, g, b, seq, tm, tf):
    t = x.shape[0]
    n_f = D_FF // tf
    return pl.pallas_call(
        functools.partial(_ffn_kernel, tm=tm, tf=tf, tiles_per_seq=seq // tm),
        grid=(t // tm, n_f),
        in_specs=[pl.BlockSpec((tm, D_MODEL), lambda i, f: (i, 0)),
                  pl.BlockSpec((None, D_MODEL, tf), lambda i, f: (layer, 0, f)),
                  pl.BlockSpec((None, D_MODEL, tf), lambda i, f: (layer, 0, n_f + f)),
                  pl.BlockSpec((None, n_f, 2, CONV_WIDTH + 1, tf), lambda i, f: (layer, 0, 0, 0, 0),
                               pipeline_mode=pl.Buffered(1)),
                  pl.BlockSpec((None, tf, D_MODEL), lambda i, f: (layer, f, 0)),
                  _const_spec(g.shape),
                  _const_spec(b.shape)],
        out_specs=pl.BlockSpec((tm, D_MODEL), lambda i, f: (i, 0)),
        out_shape=jax.ShapeDtypeStruct((t, D_MODEL), F32),
        scratch_shapes=[pltpu.VMEM((tm, D_MODEL), BF16),
                        pltpu.VMEM((n_f, 2, CONV_HALO, tf), F32)]
                       + [pltpu.VMEM((tm + CONV_HALO, FFN_SUB), F32)] * (2 * (tf // FFN_SUB)),
        compiler_params=_params(2),
        name="conv_ffn",
    )(x, w_up, w_up, cwb, w_down, g, b)


def kernel(x, mem, positions, pool_w_in, pool_w, pool_scale, diff_w_in, diff_lambda_q1, diff_lambda_k1,
           diff_lambda_q2, diff_lambda_k2, diff_subln_g, mem_w_kv, w_out, ln1_g, ln1_b, ffn_w_up,
           ffn_conv_w, ffn_conv_b, ffn_w_down, ln2_g, ln2_b):
    batch, seq, _ = x.shape
    t = batch * seq
    row = lambda a: a.reshape(1, -1)

    xs = x.reshape(t, D_MODEL)
    mem2 = mem.reshape(batch * MEM_LEN, D_MODEL)
    pos = positions.reshape(t, 1)
    inv_freq = ROPE_THETA ** (-jnp.arange(0, ROPE_DIM, 2, dtype=F32) / ROPE_DIM)
    invf = jnp.tile(jnp.tile(inv_freq, DIFF_QK_DIM // ROPE_HALF), LANES // DIFF_QK_DIM).reshape(1, LANES)

    w_out_b = w_out.astype(BF16)
    w_up_b = ffn_w_up.astype(BF16)
    w_down_b = ffn_w_down.astype(BF16)
    cwb = _conv_operands(ffn_conv_w, ffn_conv_b, FFN_CHUNK)

    for i in range(DEPTH):
        j = i // 2
        kv = _matmul(mem2, mem_w_kv[i].astype(BF16), BF16, tm=batch * MEM_LEN, tn=512)
        if i % 2 == 0:
            mix, qm = _pool(xs, pool_w_in[j].astype(BF16), pool_w[j].astype(BF16), row(pool_scale[j]),
                            batch, seq, tm=512)
        else:
            qt, kk, vt, qm = _inproj_rope(xs, diff_w_in[j].astype(BF16), pos, invf, batch, seq, tm=ATTN_TILE)
            lam_init = 0.8 - 0.6 * math.exp(-0.3 * i)
            mix = _diff_attn(qt, kk, vt, row(diff_lambda_q1[j]),
                             row(diff_lambda_k1[j]), row(diff_lambda_q2[j]), row(diff_lambda_k2[j]),
                             row(diff_subln_g[j]), batch, seq, tq=ATTN_TILE, lam_init=lam_init)
        xs = _post(mix, qm, kv, w_out_b, i, xs, row(ln1_g[i]), row(ln1_b[i]), seq, tm=512)
        xs = _ffn(xs, w_up_b, cwb, w_down_b, i, row(ln2_g[i]), row(ln2_b[i]), seq, tm=512, tf=FFN_CHUNK)
    return xs.reshape(batch, seq, D_MODEL)
```

```python
import functools
import math

import jax
import jax.numpy as jnp
from jax import lax
from jax.experimental import pallas as pl
from jax.experimental.pallas import tpu as pltpu

D_MODEL = 2048
DEPTH = 2
MEM_LEN = 256
MEM_HEADS = 4
MEM_HEAD_DIM = 128
MEM_WIDTH = MEM_HEADS * MEM_HEAD_DIM
MIX_WIDTH = D_MODEL - MEM_WIDTH
POOL_WINDOWS = (2, 4, 8, 16)
POOL_GROUP_DIM = MIX_WIDTH // len(POOL_WINDOWS)
POOL_HALO = 16
DIFF_HEADS = 12
DIFF_QK_DIM = 64
DIFF_V_DIM = 2 * DIFF_QK_DIM
VT_ROWS = DIFF_V_DIM + 16
ROPE_DIM = DIFF_QK_DIM // 4
ROPE_HALF = ROPE_DIM // 2
ROPE_THETA = 500000.0
D_FF = 5632
CONV_WIDTH = 3
CONV_HALO = 8
LN_EPS = 1e-5
RMS_EPS = 1e-6
DEEPNORM_ALPHA = (2.0 * DEPTH) ** 0.25
DIFF_IN_WIDTH = 3 * MIX_WIDTH + MEM_WIDTH
MASK_VALUE = -1e30
QK_SCALE_LOG2E = DIFF_QK_DIM ** -0.5 * math.log2(math.e)

LANES = 128
MXU_WIDTH = 256
MIX_SLABS = MIX_WIDTH // LANES
FFN_SUB = MXU_WIDTH
FFN_CHUNK = 512
ATTN_TILE = 512
VMEM_LIMIT = 56 * 1024 * 1024

BF16 = jnp.bfloat16
F32 = jnp.float32


def _params(n_axes):
    return pltpu.CompilerParams(dimension_semantics=("arbitrary",) * n_axes,
                                vmem_limit_bytes=VMEM_LIMIT)


def _const_spec(shape):
    return pl.BlockSpec(shape, lambda *_: (0,) * len(shape), pipeline_mode=pl.Buffered(1))


def _layernorm(z, g, b):
    mu = jnp.mean(z, axis=-1, keepdims=True)
    zc = z - mu
    var = jnp.mean(zc * zc, axis=-1, keepdims=True)
    return zc * lax.rsqrt(var + LN_EPS) * g + b


def _matmul_kernel(x_ref, w_ref, o_ref, xb_ref):
    @pl.when(pl.program_id(1) == 0)
    def _():
        xb_ref[...] = x_ref[...].astype(BF16)

    o_ref[...] = jnp.dot(xb_ref[...], w_ref[...], preferred_element_type=F32).astype(o_ref.dtype)


def _matmul(x, w, out_dtype, tm, tn):
    m, k = x.shape
    n = w.shape[1]
    return pl.pallas_call(
        _matmul_kernel,
        grid=(m // tm, n // tn),
        in_specs=[pl.BlockSpec((tm, k), lambda i, j: (i, 0)),
                  pl.BlockSpec((k, tn), lambda i, j: (0, j))],
        out_specs=pl.BlockSpec((tm, tn), lambda i, j: (i, j)),
        out_shape=jax.ShapeDtypeStruct((m, n), out_dtype),
        scratch_shapes=[pltpu.VMEM((tm, k), BF16)],
        compiler_params=_params(2),
        name="matmul",
    )(x, w)


def _inproj_rope_kernel(x_ref, w_ref, pos_ref, invf_ref, qt_ref, k_ref, vt_ref, qm_ref, xb_ref, tab_ref):
    xb_ref[...] = x_ref[...].astype(BF16)
    ang = pos_ref[...].astype(F32) * invf_ref[...]
    cos = jnp.cos(ang)
    sin = jnp.sin(ang)
    r = lax.broadcasted_iota(jnp.int32, ang.shape, 1) % DIFF_QK_DIM
    tab_ref[0] = jnp.where(r < ROPE_DIM, cos, 1.0)
    tab_ref[1] = jnp.where(r < ROPE_HALF, -sin, 0.0)
    tab_ref[2] = jnp.where((r >= ROPE_HALF) & (r < ROPE_DIM), sin, 0.0)

    def rope(a):
        up = pltpu.roll(a, LANES - ROPE_HALF, 1)
        dn = pltpu.roll(a, ROPE_HALF, 1)
        return a * tab_ref[0] + up * tab_ref[1] + dn * tab_ref[2]

    per = MXU_WIDTH // LANES
    for piece in range(DIFF_IN_WIDTH // MXU_WIDTH):
        acc = jnp.dot(xb_ref[...], w_ref[:, piece * MXU_WIDTH:(piece + 1) * MXU_WIDTH],
                      preferred_element_type=F32)
        for c in range(per):
            a = acc[:, c * LANES:(c + 1) * LANES]
            kind, idx = divmod(piece * per + c, DIFF_HEADS)
            if kind == 0:
                qt_ref[idx] = (rope(a) * QK_SCALE_LOG2E).T.astype(BF16)
            elif kind == 1:
                k_ref[idx] = rope(a).astype(BF16)
            elif kind == 2:
                vt_ref[idx, 0:DIFF_V_DIM, :] = a.T.astype(BF16)
                vt_ref[idx, DIFF_V_DIM:VT_ROWS, :] = jnp.ones((VT_ROWS - DIFF_V_DIM, a.shape[0]), BF16)
            else:
                qm_ref[:, idx * LANES:(idx + 1) * LANES] = a.astype(BF16)


def _inproj_rope(x, w, pos, invf, batch, seq, tm):
    m, k = x.shape
    n_s = seq // tm
    return pl.pallas_call(
        _inproj_rope_kernel,
        grid=(m // tm,),
        in_specs=[pl.BlockSpec((tm, k), lambda i: (i, 0)),
                  _const_spec(w.shape),
                  pl.BlockSpec((tm, 1), lambda i: (i, 0)),
                  _const_spec(invf.shape)],
        out_specs=[pl.BlockSpec((None, DIFF_HEADS, None, LANES, tm), lambda i: (i // n_s, 0, i % n_s, 0, 0)),
                   pl.BlockSpec((None, DIFF_HEADS, tm, LANES), lambda i: (i // n_s, 0, i % n_s, 0)),
                   pl.BlockSpec((None, DIFF_HEADS, None, VT_ROWS, tm), lambda i: (i // n_s, 0, i % n_s, 0, 0)),
                   pl.BlockSpec((tm, MEM_WIDTH), lambda i: (i, 0))],
        out_shape=[jax.ShapeDtypeStruct((batch, DIFF_HEADS, n_s, LANES, tm), BF16),
                   jax.ShapeDtypeStruct((batch, DIFF_HEADS, seq, LANES), BF16),
                   jax.ShapeDtypeStruct((batch, DIFF_HEADS, n_s, VT_ROWS, tm), BF16),
                   jax.ShapeDtypeStruct((m, MEM_WIDTH), BF16)],
        scratch_shapes=[pltpu.VMEM((tm, k), BF16), pltpu.VMEM((3, tm, LANES), F32)],
        compiler_params=_params(1),
        name="inproj_rope",
    )(x, w, pos, invf)


def _pool_kernel(x_ref, win_ref, w_ref, scale_ref, o_ref, qm_ref, ext_ref, *, tm):
    i = pl.program_id(1)

    @pl.when(i == 0)
    def _():
        ext_ref[0:POOL_HALO, :] = jnp.zeros((POOL_HALO, MIX_WIDTH), F32)

    @pl.when(i > 0)
    def _():
        ext_ref[0:POOL_HALO, :] = ext_ref[tm:tm + POOL_HALO, :]

    xb = x_ref[...].astype(BF16)
    ext_ref[POOL_HALO:POOL_HALO + tm, :] = jnp.dot(xb, win_ref[:, 0:MIX_WIDTH], preferred_element_type=F32)
    qm_ref[...] = jnp.dot(xb, win_ref[:, MIX_WIDTH:D_MODEL], preferred_element_type=F32).astype(BF16)

    pos = i * tm + lax.broadcasted_iota(jnp.int32, (tm, 1), 0)
    for g, win in enumerate(POOL_WINDOWS):
        cols = slice(g * POOL_GROUP_DIM, (g + 1) * POOL_GROUP_DIM)
        cur = ext_ref[POOL_HALO:POOL_HALO + tm, cols]
        tot = cur
        for lag in range(1, win):
            tot = tot + ext_ref[POOL_HALO - lag:POOL_HALO - lag + tm, cols]
        cnt = jnp.minimum(pos + 1, win).astype(F32)
        d = tot / cnt - cur
        y = jnp.dot(d.astype(BF16), w_ref[g], preferred_element_type=F32) * scale_ref[:, cols]
        slabs = POOL_GROUP_DIM // LANES
        for c in range(slabs):
            o_ref[g * slabs + c] = y[:, c * LANES:(c + 1) * LANES].astype(o_ref.dtype)


def _pool(x, w_in, w_groups, scale, batch, seq, tm):
    n_s = seq // tm
    return pl.pallas_call(
        functools.partial(_pool_kernel, tm=tm),
        grid=(batch, n_s),
        in_specs=[pl.BlockSpec((tm, D_MODEL), lambda b, i: (b * n_s + i, 0)),
                  _const_spec(w_in.shape),
                  _const_spec(w_groups.shape),
                  _const_spec(scale.shape)],
        out_specs=[pl.BlockSpec((None, MIX_SLABS, tm, LANES), lambda b, i: (b, 0, i, 0)),
                   pl.BlockSpec((tm, MEM_WIDTH), lambda b, i: (b * n_s + i, 0))],
        out_shape=[jax.ShapeDtypeStruct((batch, MIX_SLABS, seq, LANES), BF16),
                   jax.ShapeDtypeStruct((batch * seq, MEM_WIDTH), BF16)],
        scratch_shapes=[pltpu.VMEM((tm + POOL_HALO, MIX_WIDTH), F32)],
        compiler_params=_params(2),
        name="pool_mixer",
    )(x, w_in, w_groups, scale)


def _diff_attn_kernel(lq1_ref, lk1_ref, lq2_ref, lk2_ref, g_ref, qt_ref, k_ref, vt_ref, o_ref,
                      w_ref, sa_ref, sb_ref, m_ref, acc_ref, *, tq, lam_init):
    i = pl.program_id(2)
    tk = tq

    qt = qt_ref[...]
    row = lax.broadcasted_iota(jnp.int32, qt.shape, 0)
    zero = jnp.zeros_like(qt)
    w_ref[:, 0:tq] = jnp.where(row < DIFF_QK_DIM, qt, zero)
    w_ref[:, tq:2 * tq] = jnp.where(row >= DIFF_QK_DIM, qt, zero)
    m_ref[...] = jnp.full(m_ref.shape, MASK_VALUE, F32)
    acc_ref[...] = jnp.zeros(acc_ref.shape, F32)

    def scores(j, s_ref):
        start = pl.multiple_of(j * tk, tk)
        s_ref[...] = jnp.dot(k_ref[pl.ds(start, tk), :], w_ref[...], preferred_element_type=F32)

    def softmax_pv(j, s_ref, masked):
        s = s_ref[...]
        if masked:
            kpos = lax.broadcasted_iota(jnp.int32, s.shape, 0)
            qpos = lax.broadcasted_iota(jnp.int32, s.shape, 1) % tq
            s = jnp.where(kpos <= qpos, s, MASK_VALUE)
        m_old = m_ref[...]
        m_new = jnp.maximum(m_old, jnp.max(s, axis=0, keepdims=True))
        alpha = jnp.exp2(m_old - m_new)
        p = jnp.exp2(s - m_new).astype(BF16)
        acc_ref[...] = alpha * acc_ref[...] + jnp.dot(vt_ref[j], p, preferred_element_type=F32)
        m_ref[...] = m_new

    scores(0, sa_ref)

    def pair(jj, carry):
        j = 2 * jj
        scores(j + 1, sb_ref)
        softmax_pv(j, sa_ref, masked=False)
        scores(j + 2, sa_ref)
        softmax_pv(j + 1, sb_ref, masked=False)
        return carry

    lax.fori_loop(0, i // 2, pair, 0)

    @pl.when(i % 2 == 0)
    def _():
        softmax_pv(i, sa_ref, masked=True)

    @pl.when(i % 2 == 1)
    def _():
        scores(i, sb_ref)
        softmax_pv(i - 1, sa_ref, masked=False)
        softmax_pv(i, sb_ref, masked=True)

    lam = (jnp.exp(jnp.sum(lq1_ref[...] * lk1_ref[...], axis=-1, keepdims=True))
           - jnp.exp(jnp.sum(lq2_ref[...] * lk2_ref[...], axis=-1, keepdims=True)) + lam_init)
    norm = acc_ref[DIFF_V_DIM:DIFF_V_DIM + 1, :]
    o1 = acc_ref[0:DIFF_V_DIM, 0:tq] / norm[:, 0:tq]
    o2 = acc_ref[0:DIFF_V_DIM, tq:2 * tq] / norm[:, tq:2 * tq]
    ot = o1 - lam * o2
    ot = ot * lax.rsqrt(jnp.mean(ot * ot, axis=0, keepdims=True) + RMS_EPS)
    o_ref[...] = (ot.T * g_ref[...] * (1.0 - lam_init)).astype(o_ref.dtype)


def _diff_attn(qt, k, vt, lq1, lk1, lq2, lk2, g, batch, seq, tq, lam_init):
    n_q = seq // tq
    vec = lambda n: pl.BlockSpec((1, n), lambda b, h, i: (0, 0))
    return pl.pallas_call(
        functools.partial(_diff_attn_kernel, tq=tq, lam_init=lam_init),
        grid=(batch, DIFF_HEADS, n_q),
        in_specs=[vec(DIFF_QK_DIM), vec(DIFF_QK_DIM), vec(DIFF_QK_DIM), vec(DIFF_QK_DIM), vec(DIFF_V_DIM),
                  pl.BlockSpec((None, None, None, LANES, tq), lambda b, h, i: (b, h, i, 0, 0)),
                  pl.BlockSpec((None, None, seq, LANES), lambda b, h, i: (b, h, 0, 0)),
                  pl.BlockSpec((None, None, n_q, VT_ROWS, tq), lambda b, h, i: (b, h, 0, 0, 0))],
        out_specs=pl.BlockSpec((None, None, tq, LANES), lambda b, h, i: (b, h, i, 0)),
        out_shape=jax.ShapeDtypeStruct((batch, DIFF_HEADS, seq, LANES), BF16),
        scratch_shapes=[pltpu.VMEM((LANES, 2 * tq), BF16),
                        pltpu.VMEM((tq, 2 * tq), F32),
                        pltpu.VMEM((tq, 2 * tq), F32),
                        pltpu.VMEM((1, 2 * tq), F32),
                        pltpu.VMEM((VT_ROWS, 2 * tq), F32)],
        compiler_params=_params(3),
        name="diff_attn",
    )(lq1, lk1, lq2, lk2, g, qt, k, vt)


def _post_kernel(mix_ref, qm_ref, kv_ref, w_ref, x_ref, g_ref, b_ref, o_ref, cat_ref):
    for c in range(MIX_SLABS):
        cat_ref[:, c * LANES:(c + 1) * LANES] = mix_ref[c]
    q = qm_ref[...]
    for hd in range(MEM_HEADS):
        lo = hd * MEM_HEAD_DIM
        qh = q[:, lo:lo + MEM_HEAD_DIM]
        kh = kv_ref[:, lo:lo + MEM_HEAD_DIM]
        vh = kv_ref[:, MEM_WIDTH + lo:MEM_WIDTH + lo + MEM_HEAD_DIM]
        s = lax.dot_general(qh, kh, (((1,), (1,)), ((), ())), preferred_element_type=F32)
        s = s * (MEM_HEAD_DIM ** -0.5)
        p = jnp.exp(s - jnp.max(s, axis=-1, keepdims=True))
        denom = jnp.sum(p, axis=-1, keepdims=True)
        oh = jnp.dot(p.astype(BF16), vh, preferred_element_type=F32) / denom
        cat_ref[:, MIX_WIDTH + lo:MIX_WIDTH + lo + MEM_HEAD_DIM] = oh.astype(BF16)
    y = jnp.dot(cat_ref[...], w_ref[...], preferred_element_type=F32)
    o_ref[...] = _layernorm(DEEPNORM_ALPHA * x_ref[...] + y, g_ref[...], b_ref[...])


def _post(mix, qm, kv, w_out, layer, x, g, b, seq, tm):
    t = x.shape[0]
    n_s = seq // tm
    return pl.pallas_call(
        _post_kernel,
        grid=(t // tm,),
        in_specs=[pl.BlockSpec((None, MIX_SLABS, tm, LANES), lambda i: (i // n_s, 0, i % n_s, 0)),
                  pl.BlockSpec((tm, MEM_WIDTH), lambda i: (i, 0)),
                  pl.BlockSpec((MEM_LEN, 2 * MEM_WIDTH), lambda i: (i // n_s, 0)),
                  pl.BlockSpec((None, D_MODEL, D_MODEL), lambda i: (layer, 0, 0),
                               pipeline_mode=pl.Buffered(1)),
                  pl.BlockSpec((tm, D_MODEL), lambda i: (i, 0)),
                  _const_spec(g.shape),
                  _const_spec(b.shape)],
        out_specs=pl.BlockSpec((tm, D_MODEL), lambda i: (i, 0)),
        out_shape=jax.ShapeDtypeStruct((t, D_MODEL), F32),
        scratch_shapes=[pltpu.VMEM((tm, D_MODEL), BF16)],
        compiler_params=_params(1),
        name="post_mix",
    )(mix, qm, kv, w_out, x, g, b)


def _ffn_kernel(x_ref, wg_ref, wu_ref, cwb_ref, wd_ref, g_ref, b_ref, o_ref,
                xb_ref, carry_ref, *h_refs, tm, tf, tiles_per_seq):
    i = pl.program_id(0)
    f = pl.program_id(1)
    n_f = pl.num_programs(1)
    seq_start = (i % tiles_per_seq) == 0

    @pl.when(f == 0)
    def _():
        x = x_ref[...]
        xb_ref[...] = x.astype(BF16)
        o_ref[...] = DEEPNORM_ALPHA * x

    @pl.when((f == 0) & (i == 0))
    def _():
        carry_ref[...] = jnp.zeros(carry_ref.shape, F32)

    pieces = [slice(c * FFN_SUB, (c + 1) * FFN_SUB) for c in range(tf // FFN_SUB)]

    def project(branch, cols, h_ref):
        w_ref = (wg_ref, wu_ref)[branch]
        h = jnp.dot(xb_ref[...], w_ref[:, cols], preferred_element_type=F32)
        prev = carry_ref[f, branch, :, cols]
        h_ref[0:CONV_HALO, :] = jnp.where(seq_start, jnp.zeros_like(prev), prev)
        h_ref[CONV_HALO:CONV_HALO + tm, :] = h
        carry_ref[f, branch, :, cols] = h[tm - CONV_HALO:tm, :]

    def conv(branch, cols, h_ref):
        tap = lambda r: cwb_ref[f, branch, r:r + 1, cols]
        return (tap(CONV_WIDTH)
                + tap(2) * h_ref[CONV_HALO:CONV_HALO + tm, :]
                + tap(1) * h_ref[CONV_HALO - 1:CONV_HALO - 1 + tm, :]
                + tap(0) * h_ref[CONV_HALO - 2:CONV_HALO - 2 + tm, :])

    for c, cols in enumerate(pieces):
        project(0, cols, h_refs[2 * c])
        project(1, cols, h_refs[2 * c + 1])
    total = None
    for c, cols in enumerate(pieces):
        gate = conv(0, cols, h_refs[2 * c])
        up = conv(1, cols, h_refs[2 * c + 1])
        act = gate * jax.nn.sigmoid(gate) * up
        part = jnp.dot(act.astype(BF16), wd_ref[cols, :], preferred_element_type=F32)
        total = part if total is None else total + part
    o_ref[...] += total

    @pl.when(f == n_f - 1)
    def _():
        o_ref[...] = _layernorm(o_ref[...], g_ref[...], b_ref[...])


def _conv_operands(conv_w, conv_b, tf):
    n_l = conv_w.shape[0]
    n_f = D_FF // tf
    cwb = jnp.concatenate([conv_w, conv_b[:, None, :]], axis=1)
    return cwb.reshape(n_l, CONV_WIDTH + 1, 2, n_f, tf).transpose(0, 3, 2, 1, 4)


def _ffn(x, w_up, cwb, w_down, layer, g, b, seq, tm, tf):
    t = x.shape[0]
    n_f = D_FF // tf
    return pl.pallas_call(
        functools.partial(_ffn_kernel, tm=tm, tf=tf, tiles_per_seq=seq // tm),
        grid=(t // tm, n_f),
        in_specs=[pl.BlockSpec((tm, D_MODEL), lambda i, f: (i, 0)),
                  pl.BlockSpec((None, D_MODEL, tf), lambda i, f: (layer, 0, f)),
                  pl.BlockSpec((None, D_MODEL, tf), lambda i, f: (layer, 0, n_f + f)),
                  pl.BlockSpec((None, n_f, 2, CONV_WIDTH + 1, tf), lambda i, f: (layer, 0, 0, 0, 0),
                               pipeline_mode=pl.Buffered(1)),
                  pl.BlockSpec((None, tf, D_MODEL), lambda i, f: (layer, f, 0)),
                  _const_spec(g.shape),
                  _const_spec(b.shape)],
        out_specs=pl.BlockSpec((tm, D_MODEL), lambda i, f: (i, 0)),
        out_shape=jax.ShapeDtypeStruct((t, D_MODEL), F32),
        scratch_shapes=[pltpu.VMEM((tm, D_MODEL), BF16),
                        pltpu.VMEM((n_f, 2, CONV_HALO, tf), F32)]
                       + [pltpu.VMEM((tm + CONV_HALO, FFN_SUB), F32)] * (2 * (tf // FFN_SUB)),
        compiler_params=_params(2),
        name="conv_ffn",
    )(x, w_up, w_up, cwb, w_down, g, b)


def kernel(x, mem, positions, pool_w_in, pool_w, pool_scale, diff_w_in, diff_lambda_q1, diff_lambda_k1,
           diff_lambda_q2, diff_lambda_k2, diff_subln_g, mem_w_kv, w_out, ln1_g, ln1_b, ffn_w_up,
           ffn_conv_w, ffn_conv_b, ffn_w_down, ln2_g, ln2_b):
    batch, seq, _ = x.shape
    t = batch * seq
    row = lambda a: a.reshape(1, -1)

    xs = x.reshape(t, D_MODEL)
    mem2 = mem.reshape(batch * MEM_LEN, D_MODEL)
    pos = positions.reshape(t, 1)
    inv_freq = ROPE_THETA ** (-jnp.arange(0, ROPE_DIM, 2, dtype=F32) / ROPE_DIM)
    invf = jnp.tile(jnp.tile(inv_freq, DIFF_QK_DIM // ROPE_HALF), LANES // DIFF_QK_DIM).reshape(1, LANES)

    w_out_b = w_out.astype(BF16)
    w_up_b = ffn_w_up.astype(BF16)
    w_down_b = ffn_w_down.astype(BF16)
    cwb = _conv_operands(ffn_conv_w, ffn_conv_b, FFN_CHUNK)

    for i in range(DEPTH):
        j = i // 2
        kv = _matmul(mem2, mem_w_kv[i].astype(BF16), BF16, tm=batch * MEM_LEN, tn=512)
        if i % 2 == 0:
            mix, qm = _pool(xs, pool_w_in[j].astype(BF16), pool_w[j].astype(BF16), row(pool_scale[j]),
                            batch, seq, tm=512)
        else:
            qt, kk, vt, qm = _inproj_rope(xs, diff_w_in[j].astype(BF16), pos, invf, batch, seq, tm=ATTN_TILE)
            lam_init = 0.8 - 0.6 * math.exp(-0.3 * i)
            mix = _diff_attn(qt, kk, vt, row(diff_lambda_q1[j]),
                             row(diff_lambda_k1[j]), row(diff_lambda_q2[j]), row(diff_lambda_k2[j]),
                             row(diff_subln_g[j]), batch, seq, tq=ATTN_TILE, lam_init=lam_init)
        xs = _post(mix, qm, kv, w_out_b, i, xs, row(ln1_g[i]), row(ln1_b[i]), seq, tm=512)
        xs = _ffn(xs, w_up_b, cwb, w_down_b, i, row(ln2_g[i]), row(ln2_b[i]), seq, tm=512, tf=FFN_CHUNK)
    return xs.reshape(batch, seq, D_MODEL)
```

```python
import functools
import math

import jax
import jax.numpy as jnp
from jax import lax
from jax.experimental import pallas as pl
from jax.experimental.pallas import tpu as pltpu

D_MODEL = 2048
DEPTH = 2
MEM_LEN = 256
MEM_HEADS = 4
MEM_HEAD_DIM = 128
MEM_WIDTH = MEM_HEADS * MEM_HEAD_DIM
MIX_WIDTH = D_MODEL - MEM_WIDTH
POOL_WINDOWS = (2, 4, 8, 16)
POOL_GROUP_DIM = MIX_WIDTH // len(POOL_WINDOWS)
POOL_HALO = 16
DIFF_HEADS = 12
DIFF_QK_DIM = 64
DIFF_V_DIM = 2 * DIFF_QK_DIM
VT_ROWS = DIFF_V_DIM + 16
ROPE_DIM = DIFF_QK_DIM // 4
ROPE_HALF = ROPE_DIM // 2
ROPE_THETA = 500000.0
D_FF = 5632
CONV_WIDTH = 3
CONV_HALO = 8
LN_EPS = 1e-5
RMS_EPS = 1e-6
DEEPNORM_ALPHA = (2.0 * DEPTH) ** 0.25
DIFF_IN_WIDTH = 3 * MIX_WIDTH + MEM_WIDTH
MASK_VALUE = -1e30
QK_SCALE_LOG2E = DIFF_QK_DIM ** -0.5 * math.log2(math.e)

LANES = 128
MXU_WIDTH = 256
MIX_SLABS = MIX_WIDTH // LANES
FFN_SUB = MXU_WIDTH
FFN_CHUNK = 512
ATTN_TILE = 512
VMEM_LIMIT = 56 * 1024 * 1024

BF16 = jnp.bfloat16
F32 = jnp.float32


def _params(n_axes):
    return pltpu.CompilerParams(dimension_semantics=("arbitrary",) * n_axes,
                                vmem_limit_bytes=VMEM_LIMIT)


def _const_spec(shape):
    return pl.BlockSpec(shape, lambda *_: (0,) * len(shape), pipeline_mode=pl.Buffered(1))


def _layernorm(z, g, b):
    mu = jnp.mean(z, axis=-1, keepdims=True)
    zc = z - mu
    var = jnp.mean(zc * zc, axis=-1, keepdims=True)
    return zc * lax.rsqrt(var + LN_EPS) * g + b


def _matmul_kernel(x_ref, w_ref, o_ref, xb_ref):
    @pl.when(pl.program_id(1) == 0)
    def _():
        xb_ref[...] = x_ref[...].astype(BF16)

    o_ref[...] = jnp.dot(xb_ref[...], w_ref[...], preferred_element_type=F32).astype(o_ref.dtype)


def _matmul(x, w, out_dtype, tm, tn):
    m, k = x.shape
    n = w.shape[1]
    return pl.pallas_call(
        _matmul_kernel,
        grid=(m // tm, n // tn),
        in_specs=[pl.BlockSpec((tm, k), lambda i, j: (i, 0)),
                  pl.BlockSpec((k, tn), lambda i, j: (0, j))],
        out_specs=pl.BlockSpec((tm, tn), lambda i, j: (i, j)),
        out_shape=jax.ShapeDtypeStruct((m, n), out_dtype),
        scratch_shapes=[pltpu.VMEM((tm, k), BF16)],
        compiler_params=_params(2),
        name="matmul",
    )(x, w)


def _inproj_rope_kernel(x_ref, w_ref, pos_ref, invf_ref, qt_ref, k_ref, vt_ref, qm_ref, xb_ref, tab_ref):
    xb_ref[...] = x_ref[...].astype(BF16)
    ang = pos_ref[...].astype(F32) * invf_ref[...]
    cos = jnp.cos(ang)
    sin = jnp.sin(ang)
    r = lax.broadcasted_iota(jnp.int32, ang.shape, 1) % DIFF_QK_DIM
    tab_ref[0] = jnp.where(r < ROPE_DIM, cos, 1.0)
    tab_ref[1] = jnp.where(r < ROPE_HALF, -sin, 0.0)
    tab_ref[2] = jnp.where((r >= ROPE_HALF) & (r < ROPE_DIM), sin, 0.0)

    def rope(a):
        up = pltpu.roll(a, LANES - ROPE_HALF, 1)
        dn = pltpu.roll(a, ROPE_HALF, 1)
        return a * tab_ref[0] + up * tab_ref[1] + dn * tab_ref[2]

    per = MXU_WIDTH // LANES
    for piece in range(DIFF_IN_WIDTH // MXU_WIDTH):
        acc = jnp.dot(xb_ref[...], w_ref[:, piece * MXU_WIDTH:(piece + 1) * MXU_WIDTH],
                      preferred_element_type=F32)
        for c in range(per):
            a = acc[:, c * LANES:(c + 1) * LANES]
            kind, idx = divmod(piece * per + c, DIFF_HEADS)
            if kind == 0:
                qt_ref[idx] = (rope(a) * QK_SCALE_LOG2E).T.astype(BF16)
            elif kind == 1:
                k_ref[idx] = rope(a).astype(BF16)
            elif kind == 2:
                vt_ref[idx, 0:DIFF_V_DIM, :] = a.T.astype(BF16)
                vt_ref[idx, DIFF_V_DIM:VT_ROWS, :] = jnp.ones((VT_ROWS - DIFF_V_DIM, a.shape[0]), BF16)
            else:
                qm_ref[:, idx * LANES:(idx + 1) * LANES] = a.astype(BF16)


def _inproj_rope(x, w, pos, invf, batch, seq, tm):
    m, k = x.shape
    n_s = seq // tm
    return pl.pallas_call(
        _inproj_rope_kernel,
        grid=(m // tm,),
        in_specs=[pl.BlockSpec((tm, k), lambda i: (i, 0)),
                  _const_spec(w.shape),
                  pl.BlockSpec((tm, 1), lambda i: (i, 0)),
                  _const_spec(invf.shape)],
        out_specs=[pl.BlockSpec((None, DIFF_HEADS, None, LANES, tm), lambda i: (i // n_s, 0, i % n_s, 0, 0)),
                   pl.BlockSpec((None, DIFF_HEADS, tm, LANES), lambda i: (i // n_s, 0, i % n_s, 0)),
                   pl.BlockSpec((None, DIFF_HEADS, None, VT_ROWS, tm), lambda i: (i // n_s, 0, i % n_s, 0, 0)),
                   pl.BlockSpec((tm, MEM_WIDTH), lambda i: (i, 0))],
        out_shape=[jax.ShapeDtypeStruct((batch, DIFF_HEADS, n_s, LANES, tm), BF16),
                   jax.ShapeDtypeStruct((batch, DIFF_HEADS, seq, LANES), BF16),
                   jax.ShapeDtypeStruct((batch, DIFF_HEADS, n_s, VT_ROWS, tm), BF16),
                   jax.ShapeDtypeStruct((m, MEM_WIDTH), BF16)],
        scratch_shapes=[pltpu.VMEM((tm, k), BF16), pltpu.VMEM((3, tm, LANES), F32)],
        compiler_params=_params(1),
        name="inproj_rope",
    )(x, w, pos, invf)


def _pool_kernel(x_ref, win_ref, w_ref, scale_ref, o_ref, qm_ref, ext_ref, *, tm):
    i = pl.program_id(1)

    @pl.when(i == 0)
    def _():
        ext_ref[0:POOL_HALO, :] = jnp.zeros((POOL_HALO, MIX_WIDTH), F32)

    @pl.when(i > 0)
    def _():
        ext_ref[0:POOL_HALO, :] = ext_ref[tm:tm + POOL_HALO, :]

    xb = x_ref[...].astype(BF16)
    ext_ref[POOL_HALO:POOL_HALO + tm, :] = jnp.dot(xb, win_ref[:, 0:MIX_WIDTH], preferred_element_type=F32)
    qm_ref[...] = jnp.dot(xb, win_ref[:, MIX_WIDTH:D_MODEL], preferred_element_type=F32).astype(BF16)

    pos = i * tm + lax.broadcasted_iota(jnp.int32, (tm, 1), 0)
    for g, win in enumerate(POOL_WINDOWS):
        cols = slice(g * POOL_GROUP_DIM, (g + 1) * POOL_GROUP_DIM)
        cur = ext_ref[POOL_HALO:POOL_HALO + tm, cols]
        tot = cur
        for lag in range(1, win):
            tot = tot + ext_ref[POOL_HALO - lag:POOL_HALO - lag + tm, cols]
        cnt = jnp.minimum(pos + 1, win).astype(F32)
        d = tot / cnt - cur
        y = jnp.dot(d.astype(BF16), w_ref[g], preferred_element_type=F32) * scale_ref[:, cols]
        slabs = POOL_GROUP_DIM // LANES
        for c in range(slabs):
            o_ref[g * slabs + c] = y[:, c * LANES:(c + 1) * LANES].astype(o_ref.dtype)


def _pool(x, w_in, w_groups, scale, batch, seq, tm):
    n_s = seq // tm
    return pl.pallas_call(
        functools.partial(_pool_kernel, tm=tm),
        grid=(batch, n_s),
        in_specs=[pl.BlockSpec((tm, D_MODEL), lambda b, i: (b * n_s + i, 0)),
                  _const_spec(w_in.shape),
                  _const_spec(w_groups.shape),
                  _const_spec(scale.shape)],
        out_specs=[pl.BlockSpec((None, MIX_SLABS, tm, LANES), lambda b, i: (b, 0, i, 0)),
                   pl.BlockSpec((tm, MEM_WIDTH), lambda b, i: (b * n_s + i, 0))],
        out_shape=[jax.ShapeDtypeStruct((batch, MIX_SLABS, seq, LANES), BF16),
                   jax.ShapeDtypeStruct((batch * seq, MEM_WIDTH), BF16)],
        scratch_shapes=[pltpu.VMEM((tm + POOL_HALO, MIX_WIDTH), F32)],
        compiler_params=_params(2),
        name="pool_mixer",
    )(x, w_in, w_groups, scale)


def _diff_attn_kernel(lq1_ref, lk1_ref, lq2_ref, lk2_ref, g_ref, qt_ref, k_ref, vt_ref, o_ref,
                      w_ref, sa_ref, sb_ref, m_ref, acc_ref, *, tq, lam_init):
    i = pl.program_id(2)
    tk = tq

    qt = qt_ref[...]
    row = lax.broadcasted_iota(jnp.int32, qt.shape, 0)
    zero = jnp.zeros_like(qt)
    w_ref[:, 0:tq] = jnp.where(row < DIFF_QK_DIM, qt, zero)
    w_ref[:, tq:2 * tq] = jnp.where(row >= DIFF_QK_DIM, qt, zero)
    m_ref[...] = jnp.full(m_ref.shape, MASK_VALUE, F32)
    acc_ref[...] = jnp.zeros(acc_ref.shape, F32)

    def scores(j, s_ref):
        start = pl.multiple_of(j * tk, tk)
        s_ref[...] = jnp.dot(k_ref[pl.ds(start, tk), :], w_ref[...], preferred_element_type=F32)

    def softmax_pv(j, s_ref, masked):
        s = s_ref[...]
        if masked:
            kpos = lax.broadcasted_iota(jnp.int32, s.shape, 0)
            qpos = lax.broadcasted_iota(jnp.int32, s.shape, 1) % tq
            s = jnp.where(kpos <= qpos, s, MASK_VALUE)
        m_old = m_ref[...]
        m_new = jnp.maximum(m_old, jnp.max(s, axis=0, keepdims=True))
        alpha = jnp.exp2(m_old - m_new)
        p = jnp.exp2(s - m_new).astype(BF16)
        acc_ref[...] = alpha * acc_ref[...] + jnp.dot(vt_ref[j], p, preferred_element_type=F32)
        m_ref[...] = m_new

    scores(0, sa_ref)

    def pair(jj, carry):
        j = 2 * jj
        scores(j + 1, sb_ref)
        softmax_pv(j, sa_ref, masked=False)
        scores(j + 2, sa_ref)
        softmax_pv(j + 1, sb_ref, masked=False)
        return carry

    lax.fori_loop(0, i // 2, pair, 0)

    @pl.when(i % 2 == 0)
    def _():
        softmax_pv(i, sa_ref, masked=True)

    @pl.when(i % 2 == 1)
    def _():
        scores(i, sb_ref)
        softmax_pv(i - 1, sa_ref, masked=False)
        softmax_pv(i, sb_ref, masked=True)

    lam = (jnp.exp(jnp.sum(lq1_ref[...] * lk1_ref[...], axis=-1, keepdims=True))
           - jnp.exp(jnp.sum(lq2_ref[...] * lk2_ref[...], axis=-1, keepdims=True)) + lam_init)
    norm = acc_ref[DIFF_V_DIM:DIFF_V_DIM + 1, :]
    o1 = acc_ref[0:DIFF_V_DIM, 0:tq] / norm[:, 0:tq]
    o2 = acc_ref[0:DIFF_V_DIM, tq:2 * tq] / norm[:, tq:2 * tq]
    ot = o1 - lam * o2
    ot = ot * lax.rsqrt(jnp.mean(ot * ot, axis=0, keepdims=True) + RMS_EPS)
    o_ref[...] = (ot.T * g_ref[...] * (1.0 - lam_init)).astype(o_ref.dtype)


def _diff_attn(qt, k, vt, lq1, lk1, lq2, lk2, g, batch, seq, tq, lam_init):
    n_q = seq // tq
    vec = lambda n: pl.BlockSpec((1, n), lambda b, h, i: (0, 0))
    return pl.pallas_call(
        functools.partial(_diff_attn_kernel, tq=tq, lam_init=lam_init),
        grid=(batch, DIFF_HEADS, n_q),
        in_specs=[vec(DIFF_QK_DIM), vec(DIFF_QK_DIM), vec(DIFF_QK_DIM), vec(DIFF_QK_DIM), vec(DIFF_V_DIM),
                  pl.BlockSpec((None, None, None, LANES, tq), lambda b, h, i: (b, h, i, 0, 0)),
                  pl.BlockSpec((None, None, seq, LANES), lambda b, h, i: (b, h, 0, 0)),
                  pl.BlockSpec((None, None, n_q, VT_ROWS, tq), lambda b, h, i: (b, h, 0, 0, 0))],
        out_specs=pl.BlockSpec((None, None, tq, LANES), lambda b, h, i: (b, h, i, 0)),
        out_shape=jax.ShapeDtypeStruct((batch, DIFF_HEADS, seq, LANES), BF16),
        scratch_shapes=[pltpu.VMEM((LANES, 2 * tq), BF16),
                        pltpu.VMEM((tq, 2 * tq), F32),
                        pltpu.VMEM((tq, 2 * tq), F32),
                        pltpu.VMEM((1, 2 * tq), F32),
                        pltpu.VMEM((VT_ROWS, 2 * tq), F32)],
        compiler_params=_params(3),
        name="diff_attn",
    )(lq1, lk1, lq2, lk2, g, qt, k, vt)


def _post_kernel(mix_ref, qm_ref, kv_ref, w_ref, x_ref, g_ref, b_ref, o_ref, cat_ref):
    for c in range(MIX_SLABS):
        cat_ref[:, c * LANES:(c + 1) * LANES] = mix_ref[c]
    q = qm_ref[...]
    for hd in range(MEM_HEADS):
        lo = hd * MEM_HEAD_DIM
        qh = q[:, lo:lo + MEM_HEAD_DIM]
        kh = kv_ref[:, lo:lo + MEM_HEAD_DIM]
        vh = kv_ref[:, MEM_WIDTH + lo:MEM_WIDTH + lo + MEM_HEAD_DIM]
        s = lax.dot_general(qh, kh, (((1,), (1,)), ((), ())), preferred_element_type=F32)
        s = s * (MEM_HEAD_DIM ** -0.5)
        p = jnp.exp(s - jnp.max(s, axis=-1, keepdims=True))
        denom = jnp.sum(p, axis=-1, keepdims=True)
        oh = jnp.dot(p.astype(BF16), vh, preferred_element_type=F32) / denom
        cat_ref[:, MIX_WIDTH + lo:MIX_WIDTH + lo + MEM_HEAD_DIM] = oh.astype(BF16)
    y = jnp.dot(cat_ref[...], w_ref[...], preferred_element_type=F32)
    o_ref[...] = _layernorm(DEEPNORM_ALPHA * x_ref[...] + y, g_ref[...], b_ref[...])


def _post(mix, qm, kv, w_out, layer, x, g, b, seq, tm):
    t = x.shape[0]
    n_s = seq // tm
    return pl.pallas_call(
        _post_kernel,
        grid=(t // tm,),
        in_specs=[pl.BlockSpec((None, MIX_SLABS, tm, LANES), lambda i: (i // n_s, 0, i % n_s, 0)),
                  pl.BlockSpec((tm, MEM_WIDTH), lambda i: (i, 0)),
                  pl.BlockSpec((MEM_LEN, 2 * MEM_WIDTH), lambda i: (i // n_s, 0)),
                  pl.BlockSpec((None, D_MODEL, D_MODEL), lambda i: (layer, 0, 0),
                               pipeline_mode=pl.Buffered(1)),
                  pl.BlockSpec((tm, D_MODEL), lambda i: (i, 0)),
                  _const_spec(g.shape),
                  _const_spec(b.shape)],
        out_specs=pl.BlockSpec((tm, D_MODEL), lambda i: (i, 0)),
        out_shape=jax.ShapeDtypeStruct((t, D_MODEL), F32),
        scratch_shapes=[pltpu.VMEM((tm, D_MODEL), BF16)],
        compiler_params=_params(1),
        name="post_mix",
    )(mix, qm, kv, w_out, x, g, b)


def _ffn_kernel(x_ref, wup_hbm, cwb_ref, wdn_hbm, g_ref, b_ref, o_ref,
                xb_ref, carry_ref, wg_buf, wu_buf, wd_buf, sem, *h_refs, tm, tf, layer, tiles_per_seq):
    i = pl.program_id(0)
    n_i = pl.num_programs(0)
    n_f = D_FF // tf
    seq_start = (i % tiles_per_seq) == 0

    def chunk_copies(f, slot):
        col = pl.multiple_of(f * tf, tf)
        return (pltpu.make_async_copy(wup_hbm.at[layer, :, pl.ds(col, tf)], wg_buf.at[slot], sem.at[0, slot]),
                pltpu.make_async_copy(wup_hbm.at[layer, :, pl.ds(D_FF + col, tf)], wu_buf.at[slot],
                                      sem.at[1, slot]),
                pltpu.make_async_copy(wdn_hbm.at[layer, pl.ds(col, tf), :], wd_buf.at[slot], sem.at[2, slot]))

    @pl.when(i == 0)
    def _():
        for copy in chunk_copies(0, 0):
            copy.start()
        carry_ref[...] = jnp.zeros(carry_ref.shape, F32)

    x = x_ref[...]
    xb_ref[...] = x.astype(BF16)
    o_ref[...] = DEEPNORM_ALPHA * x

    pieces = [slice(c * FFN_SUB, (c + 1) * FFN_SUB) for c in range(tf // FFN_SUB)]
    first_slot = (i * n_f) % 2

    def chunk(f, carry):
        slot = (first_slot + f) % 2
        for copy in chunk_copies(f, slot):
            copy.wait()
        last = f == n_f - 1

        @pl.when(jnp.logical_or(jnp.logical_not(last), i + 1 < n_i))
        def _():
            for copy in chunk_copies(jnp.where(last, 0, f + 1), 1 - slot):
                copy.start()

        wg_ref, wu_ref, wd_ref = wg_buf.at[slot], wu_buf.at[slot], wd_buf.at[slot]

        def project(branch, cols, h_ref):
            w_ref = (wg_ref, wu_ref)[branch]
            h = jnp.dot(xb_ref[...], w_ref[:, cols], preferred_element_type=F32)
            prev = carry_ref[f, branch, :, cols]
            h_ref[0:CONV_HALO, :] = jnp.where(seq_start, jnp.zeros_like(prev), prev)
            h_ref[CONV_HALO:CONV_HALO + tm, :] = h
            carry_ref[f, branch, :, cols] = h[tm - CONV_HALO:tm, :]

        def conv(branch, cols, h_ref):
            tap = lambda r: cwb_ref[f, branch, r:r + 1, cols]
            return (tap(CONV_WIDTH)
                    + tap(2) * h_ref[CONV_HALO:CONV_HALO + tm, :]
                    + tap(1) * h_ref[CONV_HALO - 1:CONV_HALO - 1 + tm, :]
                    + tap(0) * h_ref[CONV_HALO - 2:CONV_HALO - 2 + tm, :])

        for c, cols in enumerate(pieces):
            project(0, cols, h_refs[2 * c])
            project(1, cols, h_refs[2 * c + 1])
        total = None
        for c, cols in enumerate(pieces):
            gate = conv(0, cols, h_refs[2 * c])
            up = conv(1, cols, h_refs[2 * c + 1])
            act = gate * jax.nn.sigmoid(gate) * up
            part = jnp.dot(act.astype(BF16), wd_ref[cols, :], preferred_element_type=F32)
            total = part if total is None else total + part
        o_ref[...] += total
        return carry

    lax.fori_loop(0, n_f, chunk, 0)
    o_ref[...] = _layernorm(o_ref[...], g_ref[...], b_ref[...])


def _conv_operands(conv_w, conv_b, tf):
    n_l = conv_w.shape[0]
    n_f = D_FF // tf
    cwb = jnp.concatenate([conv_w, conv_b[:, None, :]], axis=1)
    return cwb.reshape(n_l, CONV_WIDTH + 1, 2, n_f, tf).transpose(0, 3, 2, 1, 4)


def _ffn(x, w_up, cwb, w_down, layer, g, b, seq, tm, tf):
    t = x.shape[0]
    n_f = D_FF // tf
    return pl.pallas_call(
        functools.partial(_ffn_kernel, tm=tm, tf=tf, layer=layer, tiles_per_seq=seq // tm),
        grid=(t // tm,),
        in_specs=[pl.BlockSpec((tm, D_MODEL), lambda i: (i, 0)),
                  pl.BlockSpec(memory_space=pl.ANY),
                  pl.BlockSpec((None, n_f, 2, CONV_WIDTH + 1, tf), lambda i: (layer, 0, 0, 0, 0),
                               pipeline_mode=pl.Buffered(1)),
                  pl.BlockSpec(memory_space=pl.ANY),
                  _const_spec(g.shape),
                  _const_spec(b.shape)],
        out_specs=pl.BlockSpec((tm, D_MODEL), lambda i: (i, 0)),
        out_shape=jax.ShapeDtypeStruct((t, D_MODEL), F32),
        scratch_shapes=[pltpu.VMEM((tm, D_MODEL), BF16),
                        pltpu.VMEM((n_f, 2, CONV_HALO, tf), F32),
                        pltpu.VMEM((2, D_MODEL, tf), BF16),
                        pltpu.VMEM((2, D_MODEL, tf), BF16),
                        pltpu.VMEM((2, tf, D_MODEL), BF16),
                        pltpu.SemaphoreType.DMA((3, 2))]
                       + [pltpu.VMEM((tm + CONV_HALO, FFN_SUB), F32)] * (2 * (tf // FFN_SUB)),
        compiler_params=_params(1),
        name="conv_ffn",
    )(x, w_up, cwb, w_down, g, b)


def kernel(x, mem, positions, pool_w_in, pool_w, pool_scale, diff_w_in, diff_lambda_q1, diff_lambda_k1,
           diff_lambda_q2, diff_lambda_k2, diff_subln_g, mem_w_kv, w_out, ln1_g, ln1_b, ffn_w_up,
           ffn_conv_w, ffn_conv_b, ffn_w_down, ln2_g, ln2_b):
    batch, seq, _ = x.shape
    t = batch * seq
    row = lambda a: a.reshape(1, -1)

    xs = x.reshape(t, D_MODEL)
    mem2 = mem.reshape(batch * MEM_LEN, D_MODEL)
    pos = positions.reshape(t, 1)
    inv_freq = ROPE_THETA ** (-jnp.arange(0, ROPE_DIM, 2, dtype=F32) / ROPE_DIM)
    invf = jnp.tile(jnp.tile(inv_freq, DIFF_QK_DIM // ROPE_HALF), LANES // DIFF_QK_DIM).reshape(1, LANES)

    w_out_b = w_out.astype(BF16)
    w_up_b = ffn_w_up.astype(BF16)
    w_down_b = ffn_w_down.astype(BF16)
    cwb = _conv_operands(ffn_conv_w, ffn_conv_b, FFN_CHUNK)

    for i in range(DEPTH):
        j = i // 2
        kv = _matmul(mem2, mem_w_kv[i].astype(BF16), BF16, tm=batch * MEM_LEN, tn=512)
        if i % 2 == 0:
            mix, qm = _pool(xs, pool_w_in[j].astype(BF16), pool_w[j].astype(BF16), row(pool_scale[j]),
                            batch, seq, tm=512)
        else:
            qt, kk, vt, qm = _inproj_rope(xs, diff_w_in[j].astype(BF16), pos, invf, batch, seq, tm=ATTN_TILE)
            lam_init = 0.8 - 0.6 * math.exp(-0.3 * i)
            mix = _diff_attn(qt, kk, vt, row(diff_lambda_q1[j]),
                             row(diff_lambda_k1[j]), row(diff_lambda_q2[j]), row(diff_lambda_k2[j]),
                             row(diff_subln_g[j]), batch, seq, tq=ATTN_TILE, lam_init=lam_init)
        xs = _post(mix, qm, kv, w_out_b, i, xs, row(ln1_g[i]), row(ln1_b[i]), seq, tm=512)
        xs = _ffn(xs, w_up_b, cwb, w_down_b, i, row(ln2_g[i]), row(ln2_b[i]), seq, tm=512, tf=FFN_CHUNK)
    return xs.reshape(batch, seq, D_MODEL)
```

```python
import functools
import math

import jax
import jax.numpy as jnp
from jax import lax
from jax.experimental import pallas as pl
from jax.experimental.pallas import tpu as pltpu

D_MODEL = 2048
DEPTH = 2
MEM_LEN = 256
MEM_HEADS = 4
MEM_HEAD_DIM = 128
MEM_WIDTH = MEM_HEADS * MEM_HEAD_DIM
MIX_WIDTH = D_MODEL - MEM_WIDTH
POOL_WINDOWS = (2, 4, 8, 16)
POOL_GROUP_DIM = MIX_WIDTH // len(POOL_WINDOWS)
POOL_HALO = 16
DIFF_HEADS = 12
DIFF_QK_DIM = 64
DIFF_V_DIM = 2 * DIFF_QK_DIM
VT_ROWS = DIFF_V_DIM + 16
ROPE_DIM = DIFF_QK_DIM // 4
ROPE_HALF = ROPE_DIM // 2
ROPE_THETA = 500000.0
D_FF = 5632
CONV_WIDTH = 3
CONV_HALO = 8
LN_EPS = 1e-5
RMS_EPS = 1e-6
DEEPNORM_ALPHA = (2.0 * DEPTH) ** 0.25
DIFF_IN_WIDTH = 3 * MIX_WIDTH + MEM_WIDTH
MASK_VALUE = -1e30
QK_SCALE_LOG2E = DIFF_QK_DIM ** -0.5 * math.log2(math.e)

LANES = 128
MXU_WIDTH = 256
MIX_SLABS = MIX_WIDTH // LANES
ROW_TILE = 512
FFN_SUB = MXU_WIDTH
FFN_CHUNK = 512
ATTN_TILE = 512
VMEM_LIMIT = 56 * 1024 * 1024

BF16 = jnp.bfloat16
F32 = jnp.float32


def _params(n_axes):
    return pltpu.CompilerParams(dimension_semantics=("arbitrary",) * n_axes,
                                vmem_limit_bytes=VMEM_LIMIT)


def _const_spec(shape):
    return pl.BlockSpec(shape, lambda *_: (0,) * len(shape), pipeline_mode=pl.Buffered(1))


def _layernorm(z, g, b):
    mu = jnp.mean(z, axis=-1, keepdims=True)
    zc = z - mu
    var = jnp.mean(zc * zc, axis=-1, keepdims=True)
    return zc * lax.rsqrt(var + LN_EPS) * g + b


def _matmul_kernel(x_ref, w_ref, o_ref, xb_ref):
    @pl.when(pl.program_id(1) == 0)
    def _():
        xb_ref[...] = x_ref[...].astype(BF16)

    o_ref[...] = jnp.dot(xb_ref[...], w_ref[...], preferred_element_type=F32).astype(o_ref.dtype)


def _matmul(x, w, out_dtype, tm, tn):
    m, k = x.shape
    n = w.shape[1]
    return pl.pallas_call(
        _matmul_kernel,
        grid=(m // tm, n // tn),
        in_specs=[pl.BlockSpec((tm, k), lambda i, j: (i, 0)),
                  pl.BlockSpec((k, tn), lambda i, j: (0, j))],
        out_specs=pl.BlockSpec((tm, tn), lambda i, j: (i, j)),
        out_shape=jax.ShapeDtypeStruct((m, n), out_dtype),
        scratch_shapes=[pltpu.VMEM((tm, k), BF16)],
        compiler_params=_params(2),
        name="matmul",
    )(x, w)


def _inproj_rope_kernel(x_ref, w_ref, pos_ref, invf_ref, qt_ref, k_ref, vt_ref, qm_ref, xb_ref, tab_ref):
    xb_ref[...] = x_ref[...].astype(BF16)
    ang = pos_ref[...].astype(F32) * invf_ref[...]
    cos = jnp.cos(ang)
    sin = jnp.sin(ang)
    r = lax.broadcasted_iota(jnp.int32, ang.shape, 1) % DIFF_QK_DIM
    tab_ref[0] = jnp.where(r < ROPE_DIM, cos, 1.0)
    tab_ref[1] = jnp.where(r < ROPE_HALF, -sin, 0.0)
    tab_ref[2] = jnp.where((r >= ROPE_HALF) & (r < ROPE_DIM), sin, 0.0)

    def rope(a):
        up = pltpu.roll(a, LANES - ROPE_HALF, 1)
        dn = pltpu.roll(a, ROPE_HALF, 1)
        return a * tab_ref[0] + up * tab_ref[1] + dn * tab_ref[2]

    per = MXU_WIDTH // LANES
    for piece in range(DIFF_IN_WIDTH // MXU_WIDTH):
        acc = jnp.dot(xb_ref[...], w_ref[:, piece * MXU_WIDTH:(piece + 1) * MXU_WIDTH],
                      preferred_element_type=F32)
        for c in range(per):
            a = acc[:, c * LANES:(c + 1) * LANES]
            kind, idx = divmod(piece * per + c, DIFF_HEADS)
            if kind == 0:
                qt_ref[idx] = (rope(a) * QK_SCALE_LOG2E).T.astype(BF16)
            elif kind == 1:
                k_ref[idx] = rope(a).astype(BF16)
            elif kind == 2:
                vt_ref[idx, 0:DIFF_V_DIM, :] = a.T.astype(BF16)
                vt_ref[idx, DIFF_V_DIM:VT_ROWS, :] = jnp.ones((VT_ROWS - DIFF_V_DIM, a.shape[0]), BF16)
            else:
                qm_ref[:, idx * LANES:(idx + 1) * LANES] = a.astype(BF16)


def _inproj_rope(x, w, pos, invf, batch, seq, tm):
    m, k = x.shape
    n_s = seq // tm
    return pl.pallas_call(
        _inproj_rope_kernel,
        grid=(m // tm,),
        in_specs=[pl.BlockSpec((tm, k), lambda i: (i, 0)),
                  _const_spec(w.shape),
                  pl.BlockSpec((tm, 1), lambda i: (i, 0)),
                  _const_spec(invf.shape)],
        out_specs=[pl.BlockSpec((None, DIFF_HEADS, None, LANES, tm), lambda i: (i // n_s, 0, i % n_s, 0, 0)),
                   pl.BlockSpec((None, DIFF_HEADS, tm, LANES), lambda i: (i // n_s, 0, i % n_s, 0)),
                   pl.BlockSpec((None, DIFF_HEADS, None, VT_ROWS, tm), lambda i: (i // n_s, 0, i % n_s, 0, 0)),
                   pl.BlockSpec((tm, MEM_WIDTH), lambda i: (i, 0))],
        out_shape=[jax.ShapeDtypeStruct((batch, DIFF_HEADS, n_s, LANES, tm), BF16),
                   jax.ShapeDtypeStruct((batch, DIFF_HEADS, seq, LANES), BF16),
                   jax.ShapeDtypeStruct((batch, DIFF_HEADS, n_s, VT_ROWS, tm), BF16),
                   jax.ShapeDtypeStruct((m, MEM_WIDTH), BF16)],
        scratch_shapes=[pltpu.VMEM((tm, k), BF16), pltpu.VMEM((3, tm, LANES), F32)],
        compiler_params=_params(1),
        name="inproj_rope",
    )(x, w, pos, invf)


def _pool_kernel(x_ref, win_ref, w_ref, scale_ref, o_ref, qm_ref, ext_ref, *, tm):
    i = pl.program_id(1)

    @pl.when(i == 0)
    def _():
        ext_ref[0:POOL_HALO, :] = jnp.zeros((POOL_HALO, MIX_WIDTH), F32)

    @pl.when(i > 0)
    def _():
        ext_ref[0:POOL_HALO, :] = ext_ref[tm:tm + POOL_HALO, :]

    xb = x_ref[...].astype(BF16)
    ext_ref[POOL_HALO:POOL_HALO + tm, :] = jnp.dot(xb, win_ref[:, 0:MIX_WIDTH], preferred_element_type=F32)
    qm_ref[...] = jnp.dot(xb, win_ref[:, MIX_WIDTH:D_MODEL], preferred_element_type=F32).astype(BF16)

    pos = i * tm + lax.broadcasted_iota(jnp.int32, (tm, 1), 0)
    for g, win in enumerate(POOL_WINDOWS):
        cols = slice(g * POOL_GROUP_DIM, (g + 1) * POOL_GROUP_DIM)
        cur = ext_ref[POOL_HALO:POOL_HALO + tm, cols]
        tot = cur
        for lag in range(1, win):
            tot = tot + ext_ref[POOL_HALO - lag:POOL_HALO - lag + tm, cols]
        cnt = jnp.minimum(pos + 1, win).astype(F32)
        d = tot / cnt - cur
        y = jnp.dot(d.astype(BF16), w_ref[g], preferred_element_type=F32) * scale_ref[:, cols]
        slabs = POOL_GROUP_DIM // LANES
        for c in range(slabs):
            o_ref[g * slabs + c] = y[:, c * LANES:(c + 1) * LANES].astype(o_ref.dtype)


def _pool(x, w_in, w_groups, scale, batch, seq, tm):
    n_s = seq // tm
    return pl.pallas_call(
        functools.partial(_pool_kernel, tm=tm),
        grid=(batch, n_s),
        in_specs=[pl.BlockSpec((tm, D_MODEL), lambda b, i: (b * n_s + i, 0)),
                  _const_spec(w_in.shape),
                  _const_spec(w_groups.shape),
                  _const_spec(scale.shape)],
        out_specs=[pl.BlockSpec((None, MIX_SLABS, tm, LANES), lambda b, i: (b, 0, i, 0)),
                   pl.BlockSpec((tm, MEM_WIDTH), lambda b, i: (b * n_s + i, 0))],
        out_shape=[jax.ShapeDtypeStruct((batch, MIX_SLABS, seq, LANES), BF16),
                   jax.ShapeDtypeStruct((batch * seq, MEM_WIDTH), BF16)],
        scratch_shapes=[pltpu.VMEM((tm + POOL_HALO, MIX_WIDTH), F32)],
        compiler_params=_params(2),
        name="pool_mixer",
    )(x, w_in, w_groups, scale)


def _diff_attn_kernel(lq1_ref, lk1_ref, lq2_ref, lk2_ref, g_ref, qt_ref, k_ref, vt_ref, o_ref,
                      w_ref, sa_ref, sb_ref, m_ref, acc_ref, *, tq, lam_init):
    i = pl.program_id(2)
    tk = tq

    qt = qt_ref[...]
    row = lax.broadcasted_iota(jnp.int32, qt.shape, 0)
    zero = jnp.zeros_like(qt)
    w_ref[:, 0:tq] = jnp.where(row < DIFF_QK_DIM, qt, zero)
    w_ref[:, tq:2 * tq] = jnp.where(row >= DIFF_QK_DIM, qt, zero)
    m_ref[...] = jnp.full(m_ref.shape, MASK_VALUE, F32)
    acc_ref[...] = jnp.zeros(acc_ref.shape, F32)

    def scores(j, s_ref):
        start = pl.multiple_of(j * tk, tk)
        s_ref[...] = jnp.dot(k_ref[pl.ds(start, tk), :], w_ref[...], preferred_element_type=F32)

    def softmax_pv(j, s_ref, masked):
        s = s_ref[...]
        if masked:
            kpos = lax.broadcasted_iota(jnp.int32, s.shape, 0)
            qpos = lax.broadcasted_iota(jnp.int32, s.shape, 1) % tq
            s = jnp.where(kpos <= qpos, s, MASK_VALUE)
        m_old = m_ref[...]
        m_new = jnp.maximum(m_old, jnp.max(s, axis=0, keepdims=True))
        alpha = jnp.exp2(m_old - m_new)
        p = jnp.exp2((s - m_new).astype(BF16))
        acc_ref[...] = alpha * acc_ref[...] + jnp.dot(vt_ref[j], p, preferred_element_type=F32)
        m_ref[...] = m_new

    scores(0, sa_ref)

    def pair(jj, carry):
        j = 2 * jj
        scores(j + 1, sb_ref)
        softmax_pv(j, sa_ref, masked=False)
        scores(j + 2, sa_ref)
        softmax_pv(j + 1, sb_ref, masked=False)
        return carry

    lax.fori_loop(0, i // 2, pair, 0)

    @pl.when(i % 2 == 0)
    def _():
        softmax_pv(i, sa_ref, masked=True)

    @pl.when(i % 2 == 1)
    def _():
        scores(i, sb_ref)
        softmax_pv(i - 1, sa_ref, masked=False)
        softmax_pv(i, sb_ref, masked=True)

    lam = (jnp.exp(jnp.sum(lq1_ref[...] * lk1_ref[...], axis=-1, keepdims=True))
           - jnp.exp(jnp.sum(lq2_ref[...] * lk2_ref[...], axis=-1, keepdims=True)) + lam_init)
    norm = acc_ref[DIFF_V_DIM:DIFF_V_DIM + 1, :]
    o1 = acc_ref[0:DIFF_V_DIM, 0:tq] / norm[:, 0:tq]
    o2 = acc_ref[0:DIFF_V_DIM, tq:2 * tq] / norm[:, tq:2 * tq]
    ot = o1 - lam * o2
    ot = ot * lax.rsqrt(jnp.mean(ot * ot, axis=0, keepdims=True) + RMS_EPS)
    o_ref[...] = (ot.T * g_ref[...] * (1.0 - lam_init)).astype(o_ref.dtype)


def _diff_attn(qt, k, vt, lq1, lk1, lq2, lk2, g, batch, seq, tq, lam_init):
    n_q = seq // tq
    vec = lambda n: pl.BlockSpec((1, n), lambda b, h, i: (0, 0))
    return pl.pallas_call(
        functools.partial(_diff_attn_kernel, tq=tq, lam_init=lam_init),
        grid=(batch, DIFF_HEADS, n_q),
        in_specs=[vec(DIFF_QK_DIM), vec(DIFF_QK_DIM), vec(DIFF_QK_DIM), vec(DIFF_QK_DIM), vec(DIFF_V_DIM),
                  pl.BlockSpec((None, None, None, LANES, tq), lambda b, h, i: (b, h, i, 0, 0)),
                  pl.BlockSpec((None, None, seq, LANES), lambda b, h, i: (b, h, 0, 0)),
                  pl.BlockSpec((None, None, n_q, VT_ROWS, tq), lambda b, h, i: (b, h, 0, 0, 0))],
        out_specs=pl.BlockSpec((None, None, tq, LANES), lambda b, h, i: (b, h, i, 0)),
        out_shape=jax.ShapeDtypeStruct((batch, DIFF_HEADS, seq, LANES), BF16),
        scratch_shapes=[pltpu.VMEM((LANES, 2 * tq), BF16),
                        pltpu.VMEM((tq, 2 * tq), F32),
                        pltpu.VMEM((tq, 2 * tq), F32),
                        pltpu.VMEM((1, 2 * tq), F32),
                        pltpu.VMEM((VT_ROWS, 2 * tq), F32)],
        compiler_params=_params(3),
        name="diff_attn",
    )(lq1, lk1, lq2, lk2, g, qt, k, vt)


def _post_kernel(mix_ref, qm_ref, kv_ref, w_ref, x_ref, g_ref, b_ref, o_ref, cat_ref):
    for c in range(MIX_SLABS):
        cat_ref[:, c * LANES:(c + 1) * LANES] = mix_ref[c]
    q = qm_ref[...]
    for hd in range(MEM_HEADS):
        lo = hd * MEM_HEAD_DIM
        qh = q[:, lo:lo + MEM_HEAD_DIM]
        kh = kv_ref[:, lo:lo + MEM_HEAD_DIM]
        vh = kv_ref[:, MEM_WIDTH + lo:MEM_WIDTH + lo + MEM_HEAD_DIM]
        s = lax.dot_general(qh, kh, (((1,), (1,)), ((), ())), preferred_element_type=F32)
        s = s * (MEM_HEAD_DIM ** -0.5)
        p = jnp.exp(s - jnp.max(s, axis=-1, keepdims=True))
        denom = jnp.sum(p, axis=-1, keepdims=True)
        oh = jnp.dot(p.astype(BF16), vh, preferred_element_type=F32) / denom
        cat_ref[:, MIX_WIDTH + lo:MIX_WIDTH + lo + MEM_HEAD_DIM] = oh.astype(BF16)
    y = jnp.dot(cat_ref[...], w_ref[...], preferred_element_type=F32)
    o_ref[...] = _layernorm(DEEPNORM_ALPHA * x_ref[...] + y, g_ref[...], b_ref[...])


def _post(mix, qm, kv, w_out, layer, x, g, b, seq, tm):
    t = x.shape[0]
    n_s = seq // tm
    return pl.pallas_call(
        _post_kernel,
        grid=(t // tm,),
        in_specs=[pl.BlockSpec((None, MIX_SLABS, tm, LANES), lambda i: (i // n_s, 0, i % n_s, 0)),
                  pl.BlockSpec((tm, MEM_WIDTH), lambda i: (i, 0)),
                  pl.BlockSpec((MEM_LEN, 2 * MEM_WIDTH), lambda i: (i // n_s, 0)),
                  pl.BlockSpec((None, D_MODEL, D_MODEL), lambda i: (layer, 0, 0),
                               pipeline_mode=pl.Buffered(1)),
                  pl.BlockSpec((tm, D_MODEL), lambda i: (i, 0)),
                  _const_spec(g.shape),
                  _const_spec(b.shape)],
        out_specs=pl.BlockSpec((tm, D_MODEL), lambda i: (i, 0)),
        out_shape=jax.ShapeDtypeStruct((t, D_MODEL), F32),
        scratch_shapes=[pltpu.VMEM((tm, D_MODEL), BF16)],
        compiler_params=_params(1),
        name="post_mix",
    )(mix, qm, kv, w_out, x, g, b)


def _ffn_kernel(x_ref, wg_ref, wu_ref, cwb_ref, wd_ref, g_ref, b_ref, o_ref,
                xb_ref, carry_ref, *h_refs, tm, tf, tiles_per_seq):
    i = pl.program_id(0)
    f = pl.program_id(1)
    n_f = pl.num_programs(1)
    seq_start = (i % tiles_per_seq) == 0

    @pl.when(f == 0)
    def _():
        x = x_ref[...]
        xb_ref[...] = x.astype(BF16)
        o_ref[...] = DEEPNORM_ALPHA * x

    @pl.when((f == 0) & (i == 0))
    def _():
        carry_ref[...] = jnp.zeros(carry_ref.shape, F32)

    pieces = [slice(c * FFN_SUB, (c + 1) * FFN_SUB) for c in range(tf // FFN_SUB)]

    def project(branch, cols, h_ref):
        w_ref = (wg_ref, wu_ref)[branch]
        h = jnp.dot(xb_ref[...], w_ref[:, cols], preferred_element_type=F32)
        prev = carry_ref[f, branch, :, cols]
        h_ref[0:CONV_HALO, :] = jnp.where(seq_start, jnp.zeros_like(prev), prev)
        h_ref[CONV_HALO:CONV_HALO + tm, :] = h
        carry_ref[f, branch, :, cols] = h[tm - CONV_HALO:tm, :]

    def conv(branch, cols, h_ref):
        tap = lambda r: cwb_ref[f, branch, r:r + 1, cols]
        return (tap(CONV_WIDTH)
                + tap(2) * h_ref[CONV_HALO:CONV_HALO + tm, :]
                + tap(1) * h_ref[CONV_HALO - 1:CONV_HALO - 1 + tm, :]
                + tap(0) * h_ref[CONV_HALO - 2:CONV_HALO - 2 + tm, :])

    for c, cols in enumerate(pieces):
        project(0, cols, h_refs[2 * c])
        project(1, cols, h_refs[2 * c + 1])
    total = None
    for c, cols in enumerate(pieces):
        gate = conv(0, cols, h_refs[2 * c])
        up = conv(1, cols, h_refs[2 * c + 1])
        act = gate * jax.nn.sigmoid(gate) * up
        part = jnp.dot(act.astype(BF16), wd_ref[cols, :], preferred_element_type=F32)
        total = part if total is None else total + part
    o_ref[...] += total

    @pl.when(f == n_f - 1)
    def _():
        o_ref[...] = _layernorm(o_ref[...], g_ref[...], b_ref[...])


def _conv_operands(conv_w, conv_b, tf):
    n_l = conv_w.shape[0]
    n_f = D_FF // tf
    cwb = jnp.concatenate([conv_w, conv_b[:, None, :]], axis=1)
    return cwb.reshape(n_l, CONV_WIDTH + 1, 2, n_f, tf).transpose(0, 3, 2, 1, 4)


def _ffn(x, w_up, cwb, w_down, layer, g, b, seq, tm, tf):
    t = x.shape[0]
    n_f = D_FF // tf
    return pl.pallas_call(
        functools.partial(_ffn_kernel, tm=tm, tf=tf, tiles_per_seq=seq // tm),
        grid=(t // tm, n_f),
        in_specs=[pl.BlockSpec((tm, D_MODEL), lambda i, f: (i, 0)),
                  pl.BlockSpec((None, D_MODEL, tf), lambda i, f: (layer, 0, f)),
                  pl.BlockSpec((None, D_MODEL, tf), lambda i, f: (layer, 0, n_f + f)),
                  pl.BlockSpec((None, n_f, 2, CONV_WIDTH + 1, tf), lambda i, f: (layer, 0, 0, 0, 0),
                               pipeline_mode=pl.Buffered(1)),
                  pl.BlockSpec((None, tf, D_MODEL), lambda i, f: (layer, f, 0)),
                  _const_spec(g.shape),
                  _const_spec(b.shape)],
        out_specs=pl.BlockSpec((tm, D_MODEL), lambda i, f: (i, 0)),
        out_shape=jax.ShapeDtypeStruct((t, D_MODEL), F32),
        scratch_shapes=[pltpu.VMEM((tm, D_MODEL), BF16),
                        pltpu.VMEM((n_f, 2, CONV_HALO, tf), F32)]
                       + [pltpu.VMEM((tm + CONV_HALO, FFN_SUB), F32)] * (2 * (tf // FFN_SUB)),
        compiler_params=_params(2),
        name="conv_ffn",
    )(x, w_up, w_up, cwb, w_down, g, b)


def kernel(x, mem, positions, pool_w_in, pool_w, pool_scale, diff_w_in, diff_lambda_q1, diff_lambda_k1,
           diff_lambda_q2, diff_lambda_k2, diff_subln_g, mem_w_kv, w_out, ln1_g, ln1_b, ffn_w_up,
           ffn_conv_w, ffn_conv_b, ffn_w_down, ln2_g, ln2_b):
    batch, seq, _ = x.shape
    t = batch * seq
    row = lambda a: a.reshape(1, -1)

    xs = x.reshape(t, D_MODEL)
    mem2 = mem.reshape(batch * MEM_LEN, D_MODEL)
    pos = positions.reshape(t, 1)
    inv_freq = ROPE_THETA ** (-jnp.arange(0, ROPE_DIM, 2, dtype=F32) / ROPE_DIM)
    invf = jnp.tile(jnp.tile(inv_freq, DIFF_QK_DIM // ROPE_HALF), LANES // DIFF_QK_DIM).reshape(1, LANES)

    w_out_b = w_out.astype(BF16)
    w_up_b = ffn_w_up.astype(BF16)
    w_down_b = ffn_w_down.astype(BF16)
    cwb = _conv_operands(ffn_conv_w, ffn_conv_b, FFN_CHUNK)

    for i in range(DEPTH):
        j = i // 2
        kv = _matmul(mem2, mem_w_kv[i].astype(BF16), BF16, tm=batch * MEM_LEN, tn=MEM_WIDTH)
        if i % 2 == 0:
            mix, qm = _pool(xs, pool_w_in[j].astype(BF16), pool_w[j].astype(BF16), row(pool_scale[j]),
                            batch, seq, tm=ROW_TILE)
        else:
            qt, kk, vt, qm = _inproj_rope(xs, diff_w_in[j].astype(BF16), pos, invf, batch, seq, tm=ATTN_TILE)
            lam_init = 0.8 - 0.6 * math.exp(-0.3 * i)
            mix = _diff_attn(qt, kk, vt, row(diff_lambda_q1[j]),
                             row(diff_lambda_k1[j]), row(diff_lambda_q2[j]), row(diff_lambda_k2[j]),
                             row(diff_subln_g[j]), batch, seq, tq=ATTN_TILE, lam_init=lam_init)
        xs = _post(mix, qm, kv, w_out_b, i, xs, row(ln1_g[i]), row(ln1_b[i]), seq, tm=ROW_TILE)
        xs = _ffn(xs, w_up_b, cwb, w_down_b, i, row(ln2_g[i]), row(ln2_b[i]), seq, tm=ROW_TILE, tf=FFN_CHUNK)
    return xs.reshape(batch, seq, D_MODEL)
```

```python
import functools
import math

import jax
import jax.numpy as jnp
from jax import lax
from jax.experimental import pallas as pl
from jax.experimental.pallas import tpu as pltpu

D_MODEL = 2048
DEPTH = 2
MEM_LEN = 256
MEM_HEADS = 4
MEM_HEAD_DIM = 128
MEM_WIDTH = MEM_HEADS * MEM_HEAD_DIM
MIX_WIDTH = D_MODEL - MEM_WIDTH
POOL_WINDOWS = (2, 4, 8, 16)
POOL_GROUP_DIM = MIX_WIDTH // len(POOL_WINDOWS)
POOL_HALO = 16
DIFF_HEADS = 12
DIFF_QK_DIM = 64
DIFF_V_DIM = 2 * DIFF_QK_DIM
VT_ROWS = DIFF_V_DIM + 16
ROPE_DIM = DIFF_QK_DIM // 4
ROPE_HALF = ROPE_DIM // 2
ROPE_THETA = 500000.0
D_FF = 5632
CONV_WIDTH = 3
CONV_HALO = 8
LN_EPS = 1e-5
RMS_EPS = 1e-6
DEEPNORM_ALPHA = (2.0 * DEPTH) ** 0.25
DIFF_IN_WIDTH = 3 * MIX_WIDTH + MEM_WIDTH
MASK_VALUE = -1e30
QK_SCALE_LOG2E = DIFF_QK_DIM ** -0.5 * math.log2(math.e)

LANES = 128
MXU_WIDTH = 256
MIX_SLABS = MIX_WIDTH // LANES
ROW_TILE = 512
FFN_SUB = MXU_WIDTH
FFN_CHUNK = 512
ATTN_TILE = 512
VMEM_LIMIT = 56 * 1024 * 1024

BF16 = jnp.bfloat16
F32 = jnp.float32


def _params(n_axes):
    return pltpu.CompilerParams(dimension_semantics=("arbitrary",) * n_axes,
                                vmem_limit_bytes=VMEM_LIMIT)


def _const_spec(shape):
    return pl.BlockSpec(shape, lambda *_: (0,) * len(shape), pipeline_mode=pl.Buffered(1))


def _layernorm(z, g, b):
    mu = jnp.mean(z, axis=-1, keepdims=True)
    zc = z - mu
    var = jnp.mean(zc * zc, axis=-1, keepdims=True)
    return zc * lax.rsqrt(var + LN_EPS) * g + b


def _matmul_kernel(x_ref, w_ref, o_ref, xb_ref):
    @pl.when(pl.program_id(1) == 0)
    def _():
        xb_ref[...] = x_ref[...].astype(BF16)

    o_ref[...] = jnp.dot(xb_ref[...], w_ref[...], preferred_element_type=F32).astype(o_ref.dtype)


def _matmul(x, w, out_dtype, tm, tn):
    m, k = x.shape
    n = w.shape[1]
    return pl.pallas_call(
        _matmul_kernel,
        grid=(m // tm, n // tn),
        in_specs=[pl.BlockSpec((tm, k), lambda i, j: (i, 0)),
                  pl.BlockSpec((k, tn), lambda i, j: (0, j))],
        out_specs=pl.BlockSpec((tm, tn), lambda i, j: (i, j)),
        out_shape=jax.ShapeDtypeStruct((m, n), out_dtype),
        scratch_shapes=[pltpu.VMEM((tm, k), BF16)],
        compiler_params=_params(2),
        name="matmul",
    )(x, w)


def _inproj_rope_kernel(x_ref, w_ref, pos_ref, invf_ref, qt_ref, k_ref, vt_ref, qm_ref, xb_ref, tab_ref):
    xb_ref[...] = x_ref[...].astype(BF16)
    ang = pos_ref[...].astype(F32) * invf_ref[...]
    cos = jnp.cos(ang)
    sin = jnp.sin(ang)
    r = lax.broadcasted_iota(jnp.int32, ang.shape, 1) % DIFF_QK_DIM
    tab_ref[0] = jnp.where(r < ROPE_DIM, cos, 1.0)
    tab_ref[1] = jnp.where(r < ROPE_HALF, -sin, 0.0)
    tab_ref[2] = jnp.where((r >= ROPE_HALF) & (r < ROPE_DIM), sin, 0.0)

    def rope(a):
        up = pltpu.roll(a, LANES - ROPE_HALF, 1)
        dn = pltpu.roll(a, ROPE_HALF, 1)
        return a * tab_ref[0] + up * tab_ref[1] + dn * tab_ref[2]

    per = MXU_WIDTH // LANES
    for piece in range(DIFF_IN_WIDTH // MXU_WIDTH):
        acc = jnp.dot(xb_ref[...], w_ref[:, piece * MXU_WIDTH:(piece + 1) * MXU_WIDTH],
                      preferred_element_type=F32)
        for c in range(per):
            a = acc[:, c * LANES:(c + 1) * LANES]
            kind, idx = divmod(piece * per + c, DIFF_HEADS)
            if kind == 0:
                qt_ref[idx] = (rope(a) * QK_SCALE_LOG2E).T.astype(BF16)
            elif kind == 1:
                k_ref[idx] = rope(a).astype(BF16)
            elif kind == 2:
                vt_ref[idx, 0:DIFF_V_DIM, :] = a.T.astype(BF16)
                vt_ref[idx, DIFF_V_DIM:VT_ROWS, :] = jnp.ones((VT_ROWS - DIFF_V_DIM, a.shape[0]), BF16)
            else:
                qm_ref[:, idx * LANES:(idx + 1) * LANES] = a.astype(BF16)


def _inproj_rope(x, w, pos, invf, batch, seq, tm):
    m, k = x.shape
    n_s = seq // tm
    return pl.pallas_call(
        _inproj_rope_kernel,
        grid=(m // tm,),
        in_specs=[pl.BlockSpec((tm, k), lambda i: (i, 0)),
                  _const_spec(w.shape),
                  pl.BlockSpec((tm, 1), lambda i: (i, 0)),
                  _const_spec(invf.shape)],
        out_specs=[pl.BlockSpec((None, DIFF_HEADS, None, LANES, tm), lambda i: (i // n_s, 0, i % n_s, 0, 0)),
                   pl.BlockSpec((None, DIFF_HEADS, tm, LANES), lambda i: (i // n_s, 0, i % n_s, 0)),
                   pl.BlockSpec((None, DIFF_HEADS, None, VT_ROWS, tm), lambda i: (i // n_s, 0, i % n_s, 0, 0)),
                   pl.BlockSpec((tm, MEM_WIDTH), lambda i: (i, 0))],
        out_shape=[jax.ShapeDtypeStruct((batch, DIFF_HEADS, n_s, LANES, tm), BF16),
                   jax.ShapeDtypeStruct((batch, DIFF_HEADS, seq, LANES), BF16),
                   jax.ShapeDtypeStruct((batch, DIFF_HEADS, n_s, VT_ROWS, tm), BF16),
                   jax.ShapeDtypeStruct((m, MEM_WIDTH), BF16)],
        scratch_shapes=[pltpu.VMEM((tm, k), BF16), pltpu.VMEM((3, tm, LANES), F32)],
        compiler_params=_params(1),
        name="inproj_rope",
    )(x, w, pos, invf)


def _pool_kernel(x_ref, win_ref, w_ref, scale_ref, o_ref, qm_ref, ext_ref, *, tm):
    i = pl.program_id(1)

    @pl.when(i == 0)
    def _():
        ext_ref[0:POOL_HALO, :] = jnp.zeros((POOL_HALO, MIX_WIDTH), F32)

    @pl.when(i > 0)
    def _():
        ext_ref[0:POOL_HALO, :] = ext_ref[tm:tm + POOL_HALO, :]

    xb = x_ref[...].astype(BF16)
    ext_ref[POOL_HALO:POOL_HALO + tm, :] = jnp.dot(xb, win_ref[:, 0:MIX_WIDTH], preferred_element_type=F32)
    qm_ref[...] = jnp.dot(xb, win_ref[:, MIX_WIDTH:D_MODEL], preferred_element_type=F32).astype(BF16)

    pos = i * tm + lax.broadcasted_iota(jnp.int32, (tm, 1), 0)
    for g, win in enumerate(POOL_WINDOWS):
        cols = slice(g * POOL_GROUP_DIM, (g + 1) * POOL_GROUP_DIM)
        cur = ext_ref[POOL_HALO:POOL_HALO + tm, cols]
        tot = cur
        for lag in range(1, win):
            tot = tot + ext_ref[POOL_HALO - lag:POOL_HALO - lag + tm, cols]
        cnt = jnp.minimum(pos + 1, win).astype(F32)
        d = tot / cnt - cur
        y = jnp.dot(d.astype(BF16), w_ref[g], preferred_element_type=F32) * scale_ref[:, cols]
        slabs = POOL_GROUP_DIM // LANES
        for c in range(slabs):
            o_ref[g * slabs + c] = y[:, c * LANES:(c + 1) * LANES].astype(o_ref.dtype)


def _pool(x, w_in, w_groups, scale, batch, seq, tm):
    n_s = seq // tm
    return pl.pallas_call(
        functools.partial(_pool_kernel, tm=tm),
        grid=(batch, n_s),
        in_specs=[pl.BlockSpec((tm, D_MODEL), lambda b, i: (b * n_s + i, 0)),
                  _const_spec(w_in.shape),
                  _const_spec(w_groups.shape),
                  _const_spec(scale.shape)],
        out_specs=[pl.BlockSpec((None, MIX_SLABS, tm, LANES), lambda b, i: (b, 0, i, 0)),
                   pl.BlockSpec((tm, MEM_WIDTH), lambda b, i: (b * n_s + i, 0))],
        out_shape=[jax.ShapeDtypeStruct((batch, MIX_SLABS, seq, LANES), BF16),
                   jax.ShapeDtypeStruct((batch * seq, MEM_WIDTH), BF16)],
        scratch_shapes=[pltpu.VMEM((tm + POOL_HALO, MIX_WIDTH), F32)],
        compiler_params=_params(2),
        name="pool_mixer",
    )(x, w_in, w_groups, scale)


def _diff_attn_kernel(lq1_ref, lk1_ref, lq2_ref, lk2_ref, g_ref, qt_ref, k_ref, vt_ref, o_ref,
                      w_ref, sa_ref, sb_ref, m_ref, acc_ref, *, tq, lam_init):
    i = pl.program_id(2)
    tk = tq

    qt = qt_ref[...]
    row = lax.broadcasted_iota(jnp.int32, qt.shape, 0)
    zero = jnp.zeros_like(qt)
    w_ref[:, 0:tq] = jnp.where(row < DIFF_QK_DIM, qt, zero)
    w_ref[:, tq:2 * tq] = jnp.where(row >= DIFF_QK_DIM, qt, zero)
    m_ref[...] = jnp.full(m_ref.shape, MASK_VALUE, F32)
    acc_ref[...] = jnp.zeros(acc_ref.shape, F32)

    def scores(j, s_ref):
        start = pl.multiple_of(j * tk, tk)
        s_ref[...] = jnp.dot(k_ref[pl.ds(start, tk), :], w_ref[...], preferred_element_type=F32)

    def softmax_pv(j, s_ref, masked):
        s = s_ref[...]
        if masked:
            kpos = lax.broadcasted_iota(jnp.int32, s.shape, 0)
            qpos = lax.broadcasted_iota(jnp.int32, s.shape, 1) % tq
            s = jnp.where(kpos <= qpos, s, MASK_VALUE)
        m_old = m_ref[...]
        m_new = jnp.maximum(m_old, jnp.max(s, axis=0, keepdims=True))
        alpha = jnp.exp2(m_old - m_new)
        p = jnp.exp2(s - m_new).astype(BF16)
        acc_ref[...] = alpha * acc_ref[...] + jnp.dot(vt_ref[j], p, preferred_element_type=F32)
        m_ref[...] = m_new

    scores(0, sa_ref)

    def pair(jj):
        j = 2 * jj
        scores(j + 1, sb_ref)
        softmax_pv(j, sa_ref, masked=False)
        scores(j + 2, sa_ref)
        softmax_pv(j + 1, sb_ref, masked=False)

    def pairs_loop(first, n_trips, per_trip):
        def trip(t, carry):
            for u in range(per_trip):
                pair(first + t * per_trip + u)
            return carry
        lax.fori_loop(0, n_trips, trip, 0)

    n_pairs = i // 2
    pairs_loop(0, n_pairs // 4, 4)
    pairs_loop(4 * (n_pairs // 4), (n_pairs % 4) // 2, 2)
    pairs_loop(2 * (n_pairs // 2), n_pairs % 2, 1)

    @pl.when(i % 2 == 0)
    def _():
        softmax_pv(i, sa_ref, masked=True)

    @pl.when(i % 2 == 1)
    def _():
        scores(i, sb_ref)
        softmax_pv(i - 1, sa_ref, masked=False)
        softmax_pv(i, sb_ref, masked=True)

    lam = (jnp.exp(jnp.sum(lq1_ref[...] * lk1_ref[...], axis=-1, keepdims=True))
           - jnp.exp(jnp.sum(lq2_ref[...] * lk2_ref[...], axis=-1, keepdims=True)) + lam_init)
    norm = acc_ref[DIFF_V_DIM:DIFF_V_DIM + 1, :]
    o1 = acc_ref[0:DIFF_V_DIM, 0:tq] / norm[:, 0:tq]
    o2 = acc_ref[0:DIFF_V_DIM, tq:2 * tq] / norm[:, tq:2 * tq]
    ot = o1 - lam * o2
    ot = ot * lax.rsqrt(jnp.mean(ot * ot, axis=0, keepdims=True) + RMS_EPS)
    o_ref[...] = (ot.T * g_ref[...] * (1.0 - lam_init)).astype(o_ref.dtype)


def _diff_attn(qt, k, vt, lq1, lk1, lq2, lk2, g, batch, seq, tq, lam_init):
    n_q = seq // tq
    vec = lambda n: pl.BlockSpec((1, n), lambda b, h, i: (0, 0))
    return pl.pallas_call(
        functools.partial(_diff_attn_kernel, tq=tq, lam_init=lam_init),
        grid=(batch, DIFF_HEADS, n_q),
        in_specs=[vec(DIFF_QK_DIM), vec(DIFF_QK_DIM), vec(DIFF_QK_DIM), vec(DIFF_QK_DIM), vec(DIFF_V_DIM),
                  pl.BlockSpec((None, None, None, LANES, tq), lambda b, h, i: (b, h, i, 0, 0)),
                  pl.BlockSpec((None, None, seq, LANES), lambda b, h, i: (b, h, 0, 0)),
                  pl.BlockSpec((None, None, n_q, VT_ROWS, tq), lambda b, h, i: (b, h, 0, 0, 0))],
        out_specs=pl.BlockSpec((None, None, tq, LANES), lambda b, h, i: (b, h, i, 0)),
        out_shape=jax.ShapeDtypeStruct((batch, DIFF_HEADS, seq, LANES), BF16),
        scratch_shapes=[pltpu.VMEM((LANES, 2 * tq), BF16),
                        pltpu.VMEM((tq, 2 * tq), F32),
                        pltpu.VMEM((tq, 2 * tq), F32),
                        pltpu.VMEM((1, 2 * tq), F32),
                        pltpu.VMEM((VT_ROWS, 2 * tq), F32)],
        compiler_params=_params(3),
        name="diff_attn",
    )(lq1, lk1, lq2, lk2, g, qt, k, vt)


def _post_kernel(mix_ref, qm_ref, kv_ref, w_ref, x_ref, g_ref, b_ref, o_ref, cat_ref):
    for c in range(MIX_SLABS):
        cat_ref[:, c * LANES:(c + 1) * LANES] = mix_ref[c]
    q = qm_ref[...]
    for hd in range(MEM_HEADS):
        lo = hd * MEM_HEAD_DIM
        qh = q[:, lo:lo + MEM_HEAD_DIM]
        kh = kv_ref[:, lo:lo + MEM_HEAD_DIM]
        vh = kv_ref[:, MEM_WIDTH + lo:MEM_WIDTH + lo + MEM_HEAD_DIM]
        s = lax.dot_general(qh, kh, (((1,), (1,)), ((), ())), preferred_element_type=F32)
        s = s * (MEM_HEAD_DIM ** -0.5)
        p = jnp.exp(s - jnp.max(s, axis=-1, keepdims=True))
        denom = jnp.sum(p, axis=-1, keepdims=True)
        oh = jnp.dot(p.astype(BF16), vh, preferred_element_type=F32) / denom
        cat_ref[:, MIX_WIDTH + lo:MIX_WIDTH + lo + MEM_HEAD_DIM] = oh.astype(BF16)
    y = jnp.dot(cat_ref[...], w_ref[...], preferred_element_type=F32)
    o_ref[...] = _layernorm(DEEPNORM_ALPHA * x_ref[...] + y, g_ref[...], b_ref[...])


def _post(mix, qm, kv, w_out, layer, x, g, b, seq, tm):
    t = x.shape[0]
    n_s = seq // tm
    return pl.pallas_call(
        _post_kernel,
        grid=(t // tm,),
        in_specs=[pl.BlockSpec((None, MIX_SLABS, tm, LANES), lambda i: (i // n_s, 0, i % n_s, 0)),
                  pl.BlockSpec((tm, MEM_WIDTH), lambda i: (i, 0)),
                  pl.BlockSpec((MEM_LEN, 2 * MEM_WIDTH), lambda i: (i // n_s, 0)),
                  pl.BlockSpec((None, D_MODEL, D_MODEL), lambda i: (layer, 0, 0),
                               pipeline_mode=pl.Buffered(1)),
                  pl.BlockSpec((tm, D_MODEL), lambda i: (i, 0)),
                  _const_spec(g.shape),
                  _const_spec(b.shape)],
        out_specs=pl.BlockSpec((tm, D_MODEL), lambda i: (i, 0)),
        out_shape=jax.ShapeDtypeStruct((t, D_MODEL), F32),
        scratch_shapes=[pltpu.VMEM((tm, D_MODEL), BF16)],
        compiler_params=_params(1),
        name="post_mix",
    )(mix, qm, kv, w_out, x, g, b)


def _ffn_kernel(x_ref, wg_ref, wu_ref, cwb_ref, wd_ref, g_ref, b_ref, o_ref,
                xb_ref, carry_ref, *h_refs, tm, tf, tiles_per_seq):
    i = pl.program_id(0)
    f = pl.program_id(1)
    n_f = pl.num_programs(1)
    seq_start = (i % tiles_per_seq) == 0

    @pl.when(f == 0)
    def _():
        x = x_ref[...]
        xb_ref[...] = x.astype(BF16)
        o_ref[...] = DEEPNORM_ALPHA * x

    @pl.when((f == 0) & (i == 0))
    def _():
        carry_ref[...] = jnp.zeros(carry_ref.shape, F32)

    pieces = [slice(c * FFN_SUB, (c + 1) * FFN_SUB) for c in range(tf // FFN_SUB)]

    def project(branch, cols, h_ref):
        w_ref = (wg_ref, wu_ref)[branch]
        h = jnp.dot(xb_ref[...], w_ref[:, cols], preferred_element_type=F32)
        prev = carry_ref[f, branch, :, cols]
        h_ref[0:CONV_HALO, :] = jnp.where(seq_start, jnp.zeros_like(prev), prev)
        h_ref[CONV_HALO:CONV_HALO + tm, :] = h
        carry_ref[f, branch, :, cols] = h[tm - CONV_HALO:tm, :]

    def conv(branch, cols, h_ref):
        tap = lambda r: cwb_ref[f, branch, r:r + 1, cols]
        return (tap(CONV_WIDTH)
                + tap(2) * h_ref[CONV_HALO:CONV_HALO + tm, :]
                + tap(1) * h_ref[CONV_HALO - 1:CONV_HALO - 1 + tm, :]
                + tap(0) * h_ref[CONV_HALO - 2:CONV_HALO - 2 + tm, :])

    for c, cols in enumerate(pieces):
        project(0, cols, h_refs[2 * c])
        project(1, cols, h_refs[2 * c + 1])
    total = None
    for c, cols in enumerate(pieces):
        gate = conv(0, cols, h_refs[2 * c])
        up = conv(1, cols, h_refs[2 * c + 1])
        act = gate * jax.nn.sigmoid(gate) * up
        part = jnp.dot(act.astype(BF16), wd_ref[cols, :], preferred_element_type=F32)
        total = part if total is None else total + part
    o_ref[...] += total

    @pl.when(f == n_f - 1)
    def _():
        o_ref[...] = _layernorm(o_ref[...], g_ref[...], b_ref[...])


def _conv_operands(conv_w, conv_b, tf):
    n_l = conv_w.shape[0]
    n_f = D_FF // tf
    cwb = jnp.concatenate([conv_w, conv_b[:, None, :]], axis=1)
    return cwb.reshape(n_l, CONV_WIDTH + 1, 2, n_f, tf).transpose(0, 3, 2, 1, 4)


def _ffn(x, w_up, cwb, w_down, layer, g, b, seq, tm, tf):
    t = x.shape[0]
    n_f = D_FF // tf
    return pl.pallas_call(
        functools.partial(_ffn_kernel, tm=tm, tf=tf, tiles_per_seq=seq // tm),
        grid=(t // tm, n_f),
        in_specs=[pl.BlockSpec((tm, D_MODEL), lambda i, f: (i, 0)),
                  pl.BlockSpec((None, D_MODEL, tf), lambda i, f: (layer, 0, f)),
                  pl.BlockSpec((None, D_MODEL, tf), lambda i, f: (layer, 0, n_f + f)),
                  pl.BlockSpec((None, n_f, 2, CONV_WIDTH + 1, tf), lambda i, f: (layer, 0, 0, 0, 0),
                               pipeline_mode=pl.Buffered(1)),
                  pl.BlockSpec((None, tf, D_MODEL), lambda i, f: (layer, f, 0)),
                  _const_spec(g.shape),
                  _const_spec(b.shape)],
        out_specs=pl.BlockSpec((tm, D_MODEL), lambda i, f: (i, 0)),
        out_shape=jax.ShapeDtypeStruct((t, D_MODEL), F32),
        scratch_shapes=[pltpu.VMEM((tm, D_MODEL), BF16),
                        pltpu.VMEM((n_f, 2, CONV_HALO, tf), F32)]
                       + [pltpu.VMEM((tm + CONV_HALO, FFN_SUB), F32)] * (2 * (tf // FFN_SUB)),
        compiler_params=_params(2),
        name="conv_ffn",
    )(x, w_up, w_up, cwb, w_down, g, b)


def kernel(x, mem, positions, pool_w_in, pool_w, pool_scale, diff_w_in, diff_lambda_q1, diff_lambda_k1,
           diff_lambda_q2, diff_lambda_k2, diff_subln_g, mem_w_kv, w_out, ln1_g, ln1_b, ffn_w_up,
           ffn_conv_w, ffn_conv_b, ffn_w_down, ln2_g, ln2_b):
    batch, seq, _ = x.shape
    t = batch * seq
    row = lambda a: a.reshape(1, -1)

    xs = x.reshape(t, D_MODEL)
    mem2 = mem.reshape(batch * MEM_LEN, D_MODEL)
    pos = positions.reshape(t, 1)
    inv_freq = ROPE_THETA ** (-jnp.arange(0, ROPE_DIM, 2, dtype=F32) / ROPE_DIM)
    invf = jnp.tile(jnp.tile(inv_freq, DIFF_QK_DIM // ROPE_HALF), LANES // DIFF_QK_DIM).reshape(1, LANES)

    w_out_b = w_out.astype(BF16)
    w_up_b = ffn_w_up.astype(BF16)
    w_down_b = ffn_w_down.astype(BF16)
    cwb = _conv_operands(ffn_conv_w, ffn_conv_b, FFN_CHUNK)

    for i in range(DEPTH):
        j = i // 2
        kv = _matmul(mem2, mem_w_kv[i].astype(BF16), BF16, tm=batch * MEM_LEN, tn=MEM_WIDTH)
        if i % 2 == 0:
            mix, qm = _pool(xs, pool_w_in[j].astype(BF16), pool_w[j].astype(BF16), row(pool_scale[j]),
                            batch, seq, tm=ROW_TILE)
        else:
            qt, kk, vt, qm = _inproj_rope(xs, diff_w_in[j].astype(BF16), pos, invf, batch, seq, tm=ATTN_TILE)
            lam_init = 0.8 - 0.6 * math.exp(-0.3 * i)
            mix = _diff_attn(qt, kk, vt, row(diff_lambda_q1[j]),
                             row(diff_lambda_k1[j]), row(diff_lambda_q2[j]), row(diff_lambda_k2[j]),
                             row(diff_subln_g[j]), batch, seq, tq=ATTN_TILE, lam_init=lam_init)
        xs = _post(mix, qm, kv, w_out_b, i, xs, row(ln1_g[i]), row(ln1_b[i]), seq, tm=ROW_TILE)
        xs = _ffn(xs, w_up_b, cwb, w_down_b, i, row(ln2_g[i]), row(ln2_b[i]), seq, tm=ROW_TILE, tf=FFN_CHUNK)
    return xs.reshape(batch, seq, D_MODEL)
```

```python
import functools
import math

import jax
import jax.numpy as jnp
from jax import lax
from jax.experimental import pallas as pl
from jax.experimental.pallas import tpu as pltpu

D_MODEL = 2048
DEPTH = 2
MEM_LEN = 256
MEM_HEADS = 4
MEM_HEAD_DIM = 128
MEM_WIDTH = MEM_HEADS * MEM_HEAD_DIM
MIX_WIDTH = D_MODEL - MEM_WIDTH
POOL_WINDOWS = (2, 4, 8, 16)
POOL_GROUP_DIM = MIX_WIDTH // len(POOL_WINDOWS)
POOL_HALO = 16
DIFF_HEADS = 12
DIFF_QK_DIM = 64
DIFF_V_DIM = 2 * DIFF_QK_DIM
VT_ROWS = DIFF_V_DIM + 16
ROPE_DIM = DIFF_QK_DIM // 4
ROPE_HALF = ROPE_DIM // 2
ROPE_THETA = 500000.0
D_FF = 5632
CONV_WIDTH = 3
CONV_HALO = 8
LN_EPS = 1e-5
RMS_EPS = 1e-6
DEEPNORM_ALPHA = (2.0 * DEPTH) ** 0.25
DIFF_IN_WIDTH = 3 * MIX_WIDTH + MEM_WIDTH
MASK_VALUE = -1e30
QK_SCALE_LOG2E = DIFF_QK_DIM ** -0.5 * math.log2(math.e)

LANES = 128
MXU_WIDTH = 256
MIX_SLABS = MIX_WIDTH // LANES
ROW_TILE = 512
FFN_SUB = MXU_WIDTH
FFN_CHUNK = 512
ATTN_TILE = 512
PAIRS_PER_TRIP = (8, 4, 2, 1)
VMEM_LIMIT = 56 * 1024 * 1024

BF16 = jnp.bfloat16
F32 = jnp.float32


def _params(n_axes):
    return pltpu.CompilerParams(dimension_semantics=("arbitrary",) * n_axes,
                                vmem_limit_bytes=VMEM_LIMIT)


def _const_spec(shape):
    return pl.BlockSpec(shape, lambda *_: (0,) * len(shape), pipeline_mode=pl.Buffered(1))


def _layernorm(z, g, b):
    mu = jnp.mean(z, axis=-1, keepdims=True)
    zc = z - mu
    var = jnp.mean(zc * zc, axis=-1, keepdims=True)
    return zc * lax.rsqrt(var + LN_EPS) * g + b


def _matmul_kernel(x_ref, w_ref, o_ref, xb_ref):
    @pl.when(pl.program_id(1) == 0)
    def _():
        xb_ref[...] = x_ref[...].astype(BF16)

    o_ref[...] = jnp.dot(xb_ref[...], w_ref[...], preferred_element_type=F32).astype(o_ref.dtype)


def _matmul(x, w, out_dtype, tm, tn):
    m, k = x.shape
    n = w.shape[1]
    return pl.pallas_call(
        _matmul_kernel,
        grid=(m // tm, n // tn),
        in_specs=[pl.BlockSpec((tm, k), lambda i, j: (i, 0)),
                  pl.BlockSpec((k, tn), lambda i, j: (0, j))],
        out_specs=pl.BlockSpec((tm, tn), lambda i, j: (i, j)),
        out_shape=jax.ShapeDtypeStruct((m, n), out_dtype),
        scratch_shapes=[pltpu.VMEM((tm, k), BF16)],
        compiler_params=_params(2),
        name="matmul",
    )(x, w)


def _inproj_rope_kernel(x_ref, w_ref, pos_ref, invf_ref, qt_ref, k_ref, vt_ref, qm_ref, xb_ref, tab_ref):
    xb_ref[...] = x_ref[...].astype(BF16)
    ang = pos_ref[...].astype(F32) * invf_ref[...]
    cos = jnp.cos(ang)
    sin = jnp.sin(ang)
    r = lax.broadcasted_iota(jnp.int32, ang.shape, 1) % DIFF_QK_DIM
    tab_ref[0] = jnp.where(r < ROPE_DIM, cos, 1.0)
    tab_ref[1] = jnp.where(r < ROPE_HALF, -sin, 0.0)
    tab_ref[2] = jnp.where((r >= ROPE_HALF) & (r < ROPE_DIM), sin, 0.0)

    def rope(a):
        up = pltpu.roll(a, LANES - ROPE_HALF, 1)
        dn = pltpu.roll(a, ROPE_HALF, 1)
        return a * tab_ref[0] + up * tab_ref[1] + dn * tab_ref[2]

    per = MXU_WIDTH // LANES
    for piece in range(DIFF_IN_WIDTH // MXU_WIDTH):
        acc = jnp.dot(xb_ref[...], w_ref[:, piece * MXU_WIDTH:(piece + 1) * MXU_WIDTH],
                      preferred_element_type=F32)
        for c in range(per):
            a = acc[:, c * LANES:(c + 1) * LANES]
            kind, idx = divmod(piece * per + c, DIFF_HEADS)
            if kind == 0:
                qt_ref[idx] = (rope(a) * QK_SCALE_LOG2E).T.astype(BF16)
            elif kind == 1:
                k_ref[idx] = rope(a).astype(BF16)
            elif kind == 2:
                vt_ref[idx, 0:DIFF_V_DIM, :] = a.T.astype(BF16)
                vt_ref[idx, DIFF_V_DIM:VT_ROWS, :] = jnp.ones((VT_ROWS - DIFF_V_DIM, a.shape[0]), BF16)
            else:
                qm_ref[:, idx * LANES:(idx + 1) * LANES] = a.astype(BF16)


def _inproj_rope(x, w, pos, invf, batch, seq, tm):
    m, k = x.shape
    n_s = seq // tm
    return pl.pallas_call(
        _inproj_rope_kernel,
        grid=(m // tm,),
        in_specs=[pl.BlockSpec((tm, k), lambda i: (i, 0)),
                  _const_spec(w.shape),
                  pl.BlockSpec((tm, 1), lambda i: (i, 0)),
                  _const_spec(invf.shape)],
        out_specs=[pl.BlockSpec((None, DIFF_HEADS, None, LANES, tm), lambda i: (i // n_s, 0, i % n_s, 0, 0)),
                   pl.BlockSpec((None, DIFF_HEADS, tm, LANES), lambda i: (i // n_s, 0, i % n_s, 0)),
                   pl.BlockSpec((None, DIFF_HEADS, None, VT_ROWS, tm), lambda i: (i // n_s, 0, i % n_s, 0, 0)),
                   pl.BlockSpec((tm, MEM_WIDTH), lambda i: (i, 0))],
        out_shape=[jax.ShapeDtypeStruct((batch, DIFF_HEADS, n_s, LANES, tm), BF16),
                   jax.ShapeDtypeStruct((batch, DIFF_HEADS, seq, LANES), BF16),
                   jax.ShapeDtypeStruct((batch, DIFF_HEADS, n_s, VT_ROWS, tm), BF16),
                   jax.ShapeDtypeStruct((m, MEM_WIDTH), BF16)],
        scratch_shapes=[pltpu.VMEM((tm, k), BF16), pltpu.VMEM((3, tm, LANES), F32)],
        compiler_params=_params(1),
        name="inproj_rope",
    )(x, w, pos, invf)


def _pool_kernel(x_ref, win_ref, w_ref, scale_ref, o_ref, qm_ref, ext_ref, *, tm):
    i = pl.program_id(1)

    @pl.when(i == 0)
    def _():
        ext_ref[0:POOL_HALO, :] = jnp.zeros((POOL_HALO, MIX_WIDTH), F32)

    @pl.when(i > 0)
    def _():
        ext_ref[0:POOL_HALO, :] = ext_ref[tm:tm + POOL_HALO, :]

    xb = x_ref[...].astype(BF16)
    ext_ref[POOL_HALO:POOL_HALO + tm, :] = jnp.dot(xb, win_ref[:, 0:MIX_WIDTH], preferred_element_type=F32)
    qm_ref[...] = jnp.dot(xb, win_ref[:, MIX_WIDTH:D_MODEL], preferred_element_type=F32).astype(BF16)

    pos = i * tm + lax.broadcasted_iota(jnp.int32, (tm, 1), 0)
    for g, win in enumerate(POOL_WINDOWS):
        cols = slice(g * POOL_GROUP_DIM, (g + 1) * POOL_GROUP_DIM)
        cur = ext_ref[POOL_HALO:POOL_HALO + tm, cols]
        tot = cur
        for lag in range(1, win):
            tot = tot + ext_ref[POOL_HALO - lag:POOL_HALO - lag + tm, cols]
        cnt = jnp.minimum(pos + 1, win).astype(F32)
        d = tot / cnt - cur
        y = jnp.dot(d.astype(BF16), w_ref[g], preferred_element_type=F32) * scale_ref[:, cols]
        slabs = POOL_GROUP_DIM // LANES
        for c in range(slabs):
            o_ref[g * slabs + c] = y[:, c * LANES:(c + 1) * LANES].astype(o_ref.dtype)


def _pool(x, w_in, w_groups, scale, batch, seq, tm):
    n_s = seq // tm
    return pl.pallas_call(
        functools.partial(_pool_kernel, tm=tm),
        grid=(batch, n_s),
        in_specs=[pl.BlockSpec((tm, D_MODEL), lambda b, i: (b * n_s + i, 0)),
                  _const_spec(w_in.shape),
                  _const_spec(w_groups.shape),
                  _const_spec(scale.shape)],
        out_specs=[pl.BlockSpec((None, MIX_SLABS, tm, LANES), lambda b, i: (b, 0, i, 0)),
                   pl.BlockSpec((tm, MEM_WIDTH), lambda b, i: (b * n_s + i, 0))],
        out_shape=[jax.ShapeDtypeStruct((batch, MIX_SLABS, seq, LANES), BF16),
                   jax.ShapeDtypeStruct((batch * seq, MEM_WIDTH), BF16)],
        scratch_shapes=[pltpu.VMEM((tm + POOL_HALO, MIX_WIDTH), F32)],
        compiler_params=_params(2),
        name="pool_mixer",
    )(x, w_in, w_groups, scale)


def _diff_attn_kernel(lq1_ref, lk1_ref, lq2_ref, lk2_ref, g_ref, qt_ref, k_ref, vt_ref, o_ref,
                      w_ref, sa_ref, sb_ref, m_ref, acc_ref, *, tq, lam_init):
    i = pl.program_id(2)
    tk = tq

    qt = qt_ref[...]
    row = lax.broadcasted_iota(jnp.int32, qt.shape, 0)
    zero = jnp.zeros_like(qt)
    w_ref[:, 0:tq] = jnp.where(row < DIFF_QK_DIM, qt, zero)
    w_ref[:, tq:2 * tq] = jnp.where(row >= DIFF_QK_DIM, qt, zero)
    m_ref[...] = jnp.full(m_ref.shape, MASK_VALUE, F32)
    acc_ref[...] = jnp.zeros(acc_ref.shape, F32)

    def scores(j, s_ref):
        start = pl.multiple_of(j * tk, tk)
        s_ref[...] = jnp.dot(k_ref[pl.ds(start, tk), :], w_ref[...], preferred_element_type=F32)

    def softmax_pv(j, s_ref, masked):
        s = s_ref[...]
        if masked:
            kpos = lax.broadcasted_iota(jnp.int32, s.shape, 0)
            qpos = lax.broadcasted_iota(jnp.int32, s.shape, 1) % tq
            s = jnp.where(kpos <= qpos, s, MASK_VALUE)
        m_old = m_ref[...]
        m_new = jnp.maximum(m_old, jnp.max(s, axis=0, keepdims=True))
        alpha = jnp.exp2(m_old - m_new)
        p = jnp.exp2(s - m_new).astype(BF16)
        acc_ref[...] = alpha * acc_ref[...] + jnp.dot(vt_ref[j], p, preferred_element_type=F32)
        m_ref[...] = m_new

    scores(0, sa_ref)

    def pair(jj):
        j = 2 * jj
        scores(j + 1, sb_ref)
        softmax_pv(j, sa_ref, masked=False)
        scores(j + 2, sa_ref)
        softmax_pv(j + 1, sb_ref, masked=False)

    def pairs_loop(first, n_trips, per_trip):
        def trip(t, carry):
            for u in range(per_trip):
                pair(first + t * per_trip + u)
            return carry
        lax.fori_loop(0, n_trips, trip, 0)

    n_pairs = i // 2
    done = 0
    for per_trip in PAIRS_PER_TRIP:
        n_trips = (n_pairs - done) // per_trip
        pairs_loop(done, n_trips, per_trip)
        done = done + n_trips * per_trip

    @pl.when(i % 2 == 0)
    def _():
        softmax_pv(i, sa_ref, masked=True)

    @pl.when(i % 2 == 1)
    def _():
        scores(i, sb_ref)
        softmax_pv(i - 1, sa_ref, masked=False)
        softmax_pv(i, sb_ref, masked=True)

    lam = (jnp.exp(jnp.sum(lq1_ref[...] * lk1_ref[...], axis=-1, keepdims=True))
           - jnp.exp(jnp.sum(lq2_ref[...] * lk2_ref[...], axis=-1, keepdims=True)) + lam_init)
    norm = acc_ref[DIFF_V_DIM:DIFF_V_DIM + 1, :]
    o1 = acc_ref[0:DIFF_V_DIM, 0:tq] / norm[:, 0:tq]
    o2 = acc_ref[0:DIFF_V_DIM, tq:2 * tq] / norm[:, tq:2 * tq]
    ot = o1 - lam * o2
    ot = ot * lax.rsqrt(jnp.mean(ot * ot, axis=0, keepdims=True) + RMS_EPS)
    o_ref[...] = (ot.T * g_ref[...] * (1.0 - lam_init)).astype(o_ref.dtype)


def _diff_attn(qt, k, vt, lq1, lk1, lq2, lk2, g, batch, seq, tq, lam_init):
    n_q = seq // tq
    vec = lambda n: pl.BlockSpec((1, n), lambda b, h, i: (0, 0))
    return pl.pallas_call(
        functools.partial(_diff_attn_kernel, tq=tq, lam_init=lam_init),
        grid=(batch, DIFF_HEADS, n_q),
        in_specs=[vec(DIFF_QK_DIM), vec(DIFF_QK_DIM), vec(DIFF_QK_DIM), vec(DIFF_QK_DIM), vec(DIFF_V_DIM),
                  pl.BlockSpec((None, None, None, LANES, tq), lambda b, h, i: (b, h, i, 0, 0)),
                  pl.BlockSpec((None, None, seq, LANES), lambda b, h, i: (b, h, 0, 0)),
                  pl.BlockSpec((None, None, n_q, VT_ROWS, tq), lambda b, h, i: (b, h, 0, 0, 0))],
        out_specs=pl.BlockSpec((None, None, tq, LANES), lambda b, h, i: (b, h, i, 0)),
        out_shape=jax.ShapeDtypeStruct((batch, DIFF_HEADS, seq, LANES), BF16),
        scratch_shapes=[pltpu.VMEM((LANES, 2 * tq), BF16),
                        pltpu.VMEM((tq, 2 * tq), F32),
                        pltpu.VMEM((tq, 2 * tq), F32),
                        pltpu.VMEM((1, 2 * tq), F32),
                        pltpu.VMEM((VT_ROWS, 2 * tq), F32)],
        compiler_params=_params(3),
        name="diff_attn",
    )(lq1, lk1, lq2, lk2, g, qt, k, vt)


def _post_kernel(mix_ref, qm_ref, kv_ref, w_ref, x_ref, g_ref, b_ref, o_ref, cat_ref):
    for c in range(MIX_SLABS):
        cat_ref[:, c * LANES:(c + 1) * LANES] = mix_ref[c]
    q = qm_ref[...]
    for hd in range(MEM_HEADS):
        lo = hd * MEM_HEAD_DIM
        qh = q[:, lo:lo + MEM_HEAD_DIM]
        kh = kv_ref[:, lo:lo + MEM_HEAD_DIM]
        vh = kv_ref[:, MEM_WIDTH + lo:MEM_WIDTH + lo + MEM_HEAD_DIM]
        s = lax.dot_general(qh, kh, (((1,), (1,)), ((), ())), preferred_element_type=F32)
        s = s * (MEM_HEAD_DIM ** -0.5)
        p = jnp.exp(s - jnp.max(s, axis=-1, keepdims=True))
        denom = jnp.sum(p, axis=-1, keepdims=True)
        oh = jnp.dot(p.astype(BF16), vh, preferred_element_type=F32) / denom
        cat_ref[:, MIX_WIDTH + lo:MIX_WIDTH + lo + MEM_HEAD_DIM] = oh.astype(BF16)
    y = jnp.dot(cat_ref[...], w_ref[...], preferred_element_type=F32)
    o_ref[...] = _layernorm(DEEPNORM_ALPHA * x_ref[...] + y, g_ref[...], b_ref[...])


def _post(mix, qm, kv, w_out, layer, x, g, b, seq, tm):
    t = x.shape[0]
    n_s = seq // tm
    return pl.pallas_call(
        _post_kernel,
        grid=(t // tm,),
        in_specs=[pl.BlockSpec((None, MIX_SLABS, tm, LANES), lambda i: (i // n_s, 0, i % n_s, 0)),
                  pl.BlockSpec((tm, MEM_WIDTH), lambda i: (i, 0)),
                  pl.BlockSpec((MEM_LEN, 2 * MEM_WIDTH), lambda i: (i // n_s, 0)),
                  pl.BlockSpec((None, D_MODEL, D_MODEL), lambda i: (layer, 0, 0),
                               pipeline_mode=pl.Buffered(1)),
                  pl.BlockSpec((tm, D_MODEL), lambda i: (i, 0)),
                  _const_spec(g.shape),
                  _const_spec(b.shape)],
        out_specs=pl.BlockSpec((tm, D_MODEL), lambda i: (i, 0)),
        out_shape=jax.ShapeDtypeStruct((t, D_MODEL), F32),
        scratch_shapes=[pltpu.VMEM((tm, D_MODEL), BF16)],
        compiler_params=_params(1),
        name="post_mix",
    )(mix, qm, kv, w_out, x, g, b)


def _ffn_kernel(x_ref, wg_ref, wu_ref, cwb_ref, wd_ref, g_ref, b_ref, o_ref,
                xb_ref, carry_ref, *h_refs, tm, tf, tiles_per_seq):
    i = pl.program_id(0)
    f = pl.program_id(1)
    n_f = pl.num_programs(1)
    seq_start = (i % tiles_per_seq) == 0

    @pl.when(f == 0)
    def _():
        x = x_ref[...]
        xb_ref[...] = x.astype(BF16)
        o_ref[...] = DEEPNORM_ALPHA * x

    @pl.when((f == 0) & (i == 0))
    def _():
        carry_ref[...] = jnp.zeros(carry_ref.shape, F32)

    pieces = [slice(c * FFN_SUB, (c + 1) * FFN_SUB) for c in range(tf // FFN_SUB)]

    def project(branch, cols, h_ref):
        w_ref = (wg_ref, wu_ref)[branch]
        h = jnp.dot(xb_ref[...], w_ref[:, cols], preferred_element_type=F32)
        prev = carry_ref[f, branch, :, cols]
        h_ref[0:CONV_HALO, :] = jnp.where(seq_start, jnp.zeros_like(prev), prev)
        h_ref[CONV_HALO:CONV_HALO + tm, :] = h
        carry_ref[f, branch, :, cols] = h[tm - CONV_HALO:tm, :]

    def conv(branch, cols, h_ref):
        tap = lambda r: cwb_ref[f, branch, r:r + 1, cols]
        return (tap(CONV_WIDTH)
                + tap(2) * h_ref[CONV_HALO:CONV_HALO + tm, :]
                + tap(1) * h_ref[CONV_HALO - 1:CONV_HALO - 1 + tm, :]
                + tap(0) * h_ref[CONV_HALO - 2:CONV_HALO - 2 + tm, :])

    for c, cols in enumerate(pieces):
        project(0, cols, h_refs[2 * c])
        project(1, cols, h_refs[2 * c + 1])
    total = None
    for c, cols in enumerate(pieces):
        gate = conv(0, cols, h_refs[2 * c])
        up = conv(1, cols, h_refs[2 * c + 1])
        act = gate * jax.nn.sigmoid(gate) * up
        part = jnp.dot(act.astype(BF16), wd_ref[cols, :], preferred_element_type=F32)
        total = part if total is None else total + part
    o_ref[...] += total

    @pl.when(f == n_f - 1)
    def _():
        o_ref[...] = _layernorm(o_ref[...], g_ref[...], b_ref[...])


def _conv_operands(conv_w, conv_b, tf):
    n_l = conv_w.shape[0]
    n_f = D_FF // tf
    cwb = jnp.concatenate([conv_w, conv_b[:, None, :]], axis=1)
    return cwb.reshape(n_l, CONV_WIDTH + 1, 2, n_f, tf).transpose(0, 3, 2, 1, 4)


def _ffn(x, w_up, cwb, w_down, layer, g, b, seq, tm, tf):
    t = x.shape[0]
    n_f = D_FF // tf
    return pl.pallas_call(
        functools.partial(_ffn_kernel, tm=tm, tf=tf, tiles_per_seq=seq // tm),
        grid=(t // tm, n_f),
        in_specs=[pl.BlockSpec((tm, D_MODEL), lambda i, f: (i, 0)),
                  pl.BlockSpec((None, D_MODEL, tf), lambda i, f: (layer, 0, f)),
                  pl.BlockSpec((None, D_MODEL, tf), lambda i, f: (layer, 0, n_f + f)),
                  pl.BlockSpec((None, n_f, 2, CONV_WIDTH + 1, tf), lambda i, f: (layer, 0, 0, 0, 0),
                               pipeline_mode=pl.Buffered(1)),
                  pl.BlockSpec((None, tf, D_MODEL), lambda i, f: (layer, f, 0)),
                  _const_spec(g.shape),
                  _const_spec(b.shape)],
        out_specs=pl.BlockSpec((tm, D_MODEL), lambda i, f: (i, 0)),
        out_shape=jax.ShapeDtypeStruct((t, D_MODEL), F32),
        scratch_shapes=[pltpu.VMEM((tm, D_MODEL), BF16),
                        pltpu.VMEM((n_f, 2, CONV_HALO, tf), F32)]
                       + [pltpu.VMEM((tm + CONV_HALO, FFN_SUB), F32)] * (2 * (tf // FFN_SUB)),
        compiler_params=_params(2),
        name="conv_ffn",
    )(x, w_up, w_up, cwb, w_down, g, b)


def kernel(x, mem, positions, pool_w_in, pool_w, pool_scale, diff_w_in, diff_lambda_q1, diff_lambda_k1,
           diff_lambda_q2, diff_lambda_k2, diff_subln_g, mem_w_kv, w_out, ln1_g, ln1_b, ffn_w_up,
           ffn_conv_w, ffn_conv_b, ffn_w_down, ln2_g, ln2_b):
    batch, seq, _ = x.shape
    t = batch * seq
    row = lambda a: a.reshape(1, -1)

    xs = x.reshape(t, D_MODEL)
    mem2 = mem.reshape(batch * MEM_LEN, D_MODEL)
    pos = positions.reshape(t, 1)
    inv_freq = ROPE_THETA ** (-jnp.arange(0, ROPE_DIM, 2, dtype=F32) / ROPE_DIM)
    invf = jnp.tile(jnp.tile(inv_freq, DIFF_QK_DIM // ROPE_HALF), LANES // DIFF_QK_DIM).reshape(1, LANES)

    w_out_b = w_out.astype(BF16)
    w_up_b = ffn_w_up.astype(BF16)
    w_down_b = ffn_w_down.astype(BF16)
    cwb = _conv_operands(ffn_conv_w, ffn_conv_b, FFN_CHUNK)

    for i in range(DEPTH):
        j = i // 2
        kv = _matmul(mem2, mem_w_kv[i].astype(BF16), BF16, tm=batch * MEM_LEN, tn=MEM_WIDTH)
        if i % 2 == 0:
            mix, qm = _pool(xs, pool_w_in[j].astype(BF16), pool_w[j].astype(BF16), row(pool_scale[j]),
                            batch, seq, tm=ROW_TILE)
        else:
            qt, kk, vt, qm = _inproj_rope(xs, diff_w_in[j].astype(BF16), pos, invf, batch, seq, tm=ATTN_TILE)
            lam_init = 0.8 - 0.6 * math.exp(-0.3 * i)
            mix = _diff_attn(qt, kk, vt, row(diff_lambda_q1[j]),
                             row(diff_lambda_k1[j]), row(diff_lambda_q2[j]), row(diff_lambda_k2[j]),
                             row(diff_subln_g[j]), batch, seq, tq=ATTN_TILE, lam_init=lam_init)
        xs = _post(mix, qm, kv, w_out_b, i, xs, row(ln1_g[i]), row(ln1_b[i]), seq, tm=ROW_TILE)
        xs = _ffn(xs, w_up_b, cwb, w_down_b, i, row(ln2_g[i]), row(ln2_b[i]), seq, tm=ROW_TILE, tf=FFN_CHUNK)
    return xs.reshape(batch, seq, D_MODEL)
```

```python
import functools
import math

import jax
import jax.numpy as jnp
from jax import lax
from jax.experimental import pallas as pl
from jax.experimental.pallas import tpu as pltpu

D_MODEL = 2048
DEPTH = 2
MEM_LEN = 256
MEM_HEADS = 4
MEM_HEAD_DIM = 128
MEM_WIDTH = MEM_HEADS * MEM_HEAD_DIM
MIX_WIDTH = D_MODEL - MEM_WIDTH
POOL_WINDOWS = (2, 4, 8, 16)
POOL_GROUP_DIM = MIX_WIDTH // len(POOL_WINDOWS)
POOL_HALO = 16
DIFF_HEADS = 12
DIFF_QK_DIM = 64
DIFF_V_DIM = 2 * DIFF_QK_DIM
VT_ROWS = DIFF_V_DIM + 16
ROPE_DIM = DIFF_QK_DIM // 4
ROPE_HALF = ROPE_DIM // 2
ROPE_THETA = 500000.0
D_FF = 5632
CONV_WIDTH = 3
CONV_HALO = 8
LN_EPS = 1e-5
RMS_EPS = 1e-6
DEEPNORM_ALPHA = (2.0 * DEPTH) ** 0.25
DIFF_IN_WIDTH = 3 * MIX_WIDTH + MEM_WIDTH
MASK_VALUE = -1e30
QK_SCALE_LOG2E = DIFF_QK_DIM ** -0.5 * math.log2(math.e)

LANES = 128
MXU_WIDTH = 256
MIX_SLABS = MIX_WIDTH // LANES
ROW_TILE = 512
FFN_SUB = MXU_WIDTH
FFN_CHUNK = 512
ATTN_TILE = 512
PAIRS_PER_TRIP = (8, 4, 2, 1)
VMEM_LIMIT = 56 * 1024 * 1024
FFN_VMEM_LIMIT = 60 * 1024 * 1024

BF16 = jnp.bfloat16
F32 = jnp.float32


def _params(n_axes, vmem_limit=VMEM_LIMIT):
    return pltpu.CompilerParams(dimension_semantics=("arbitrary",) * n_axes,
                                vmem_limit_bytes=vmem_limit)


def _const_spec(shape):
    return pl.BlockSpec(shape, lambda *_: (0,) * len(shape), pipeline_mode=pl.Buffered(1))


def _layernorm(z, g, b):
    mu = jnp.mean(z, axis=-1, keepdims=True)
    zc = z - mu
    var = jnp.mean(zc * zc, axis=-1, keepdims=True)
    return zc * lax.rsqrt(var + LN_EPS) * g + b


def _matmul_kernel(x_ref, w_ref, o_ref, xb_ref):
    @pl.when(pl.program_id(1) == 0)
    def _():
        xb_ref[...] = x_ref[...].astype(BF16)

    o_ref[...] = jnp.dot(xb_ref[...], w_ref[...], preferred_element_type=F32).astype(o_ref.dtype)


def _matmul(x, w, out_dtype, tm, tn):
    m, k = x.shape
    n = w.shape[1]
    return pl.pallas_call(
        _matmul_kernel,
        grid=(m // tm, n // tn),
        in_specs=[pl.BlockSpec((tm, k), lambda i, j: (i, 0)),
                  pl.BlockSpec((k, tn), lambda i, j: (0, j))],
        out_specs=pl.BlockSpec((tm, tn), lambda i, j: (i, j)),
        out_shape=jax.ShapeDtypeStruct((m, n), out_dtype),
        scratch_shapes=[pltpu.VMEM((tm, k), BF16)],
        compiler_params=_params(2),
        name="matmul",
    )(x, w)


def _inproj_rope_kernel(x_ref, w_ref, pos_ref, invf_ref, qt_ref, k_ref, vt_ref, qm_ref, xb_ref, tab_ref):
    xb_ref[...] = x_ref[...].astype(BF16)
    ang = pos_ref[...].astype(F32) * invf_ref[...]
    cos = jnp.cos(ang)
    sin = jnp.sin(ang)
    r = lax.broadcasted_iota(jnp.int32, ang.shape, 1) % DIFF_QK_DIM
    tab_ref[0] = jnp.where(r < ROPE_DIM, cos, 1.0)
    tab_ref[1] = jnp.where(r < ROPE_HALF, -sin, 0.0)
    tab_ref[2] = jnp.where((r >= ROPE_HALF) & (r < ROPE_DIM), sin, 0.0)

    def rope(a):
        up = pltpu.roll(a, LANES - ROPE_HALF, 1)
        dn = pltpu.roll(a, ROPE_HALF, 1)
        return a * tab_ref[0] + up * tab_ref[1] + dn * tab_ref[2]

    per = MXU_WIDTH // LANES
    for piece in range(DIFF_IN_WIDTH // MXU_WIDTH):
        acc = jnp.dot(xb_ref[...], w_ref[:, piece * MXU_WIDTH:(piece + 1) * MXU_WIDTH],
                      preferred_element_type=F32)
        for c in range(per):
            a = acc[:, c * LANES:(c + 1) * LANES]
            kind, idx = divmod(piece * per + c, DIFF_HEADS)
            if kind == 0:
                qt_ref[idx] = (rope(a) * QK_SCALE_LOG2E).T.astype(BF16)
            elif kind == 1:
                k_ref[idx] = rope(a).astype(BF16)
            elif kind == 2:
                vt_ref[idx, 0:DIFF_V_DIM, :] = a.T.astype(BF16)
                vt_ref[idx, DIFF_V_DIM:VT_ROWS, :] = jnp.ones((VT_ROWS - DIFF_V_DIM, a.shape[0]), BF16)
            else:
                qm_ref[:, idx * LANES:(idx + 1) * LANES] = a.astype(BF16)


def _inproj_rope(x, w, pos, invf, batch, seq, tm):
    m, k = x.shape
    n_s = seq // tm
    return pl.pallas_call(
        _inproj_rope_kernel,
        grid=(m // tm,),
        in_specs=[pl.BlockSpec((tm, k), lambda i: (i, 0)),
                  _const_spec(w.shape),
                  pl.BlockSpec((tm, 1), lambda i: (i, 0)),
                  _const_spec(invf.shape)],
        out_specs=[pl.BlockSpec((None, DIFF_HEADS, None, LANES, tm), lambda i: (i // n_s, 0, i % n_s, 0, 0)),
                   pl.BlockSpec((None, DIFF_HEADS, tm, LANES), lambda i: (i // n_s, 0, i % n_s, 0)),
                   pl.BlockSpec((None, DIFF_HEADS, None, VT_ROWS, tm), lambda i: (i // n_s, 0, i % n_s, 0, 0)),
                   pl.BlockSpec((tm, MEM_WIDTH), lambda i: (i, 0))],
        out_shape=[jax.ShapeDtypeStruct((batch, DIFF_HEADS, n_s, LANES, tm), BF16),
                   jax.ShapeDtypeStruct((batch, DIFF_HEADS, seq, LANES), BF16),
                   jax.ShapeDtypeStruct((batch, DIFF_HEADS, n_s, VT_ROWS, tm), BF16),
                   jax.ShapeDtypeStruct((m, MEM_WIDTH), BF16)],
        scratch_shapes=[pltpu.VMEM((tm, k), BF16), pltpu.VMEM((3, tm, LANES), F32)],
        compiler_params=_params(1),
        name="inproj_rope",
    )(x, w, pos, invf)


def _pool_kernel(x_ref, win_ref, w_ref, scale_ref, o_ref, qm_ref, ext_ref, *, tm):
    i = pl.program_id(1)

    @pl.when(i == 0)
    def _():
        ext_ref[0:POOL_HALO, :] = jnp.zeros((POOL_HALO, MIX_WIDTH), F32)

    @pl.when(i > 0)
    def _():
        ext_ref[0:POOL_HALO, :] = ext_ref[tm:tm + POOL_HALO, :]

    xb = x_ref[...].astype(BF16)
    ext_ref[POOL_HALO:POOL_HALO + tm, :] = jnp.dot(xb, win_ref[:, 0:MIX_WIDTH], preferred_element_type=F32)
    qm_ref[...] = jnp.dot(xb, win_ref[:, MIX_WIDTH:D_MODEL], preferred_element_type=F32).astype(BF16)

    pos = i * tm + lax.broadcasted_iota(jnp.int32, (tm, 1), 0)
    for g, win in enumerate(POOL_WINDOWS):
        cols = slice(g * POOL_GROUP_DIM, (g + 1) * POOL_GROUP_DIM)
        cur = ext_ref[POOL_HALO:POOL_HALO + tm, cols]
        tot = cur
        for lag in range(1, win):
            tot = tot + ext_ref[POOL_HALO - lag:POOL_HALO - lag + tm, cols]
        cnt = jnp.minimum(pos + 1, win).astype(F32)
        d = tot / cnt - cur
        y = jnp.dot(d.astype(BF16), w_ref[g], preferred_element_type=F32) * scale_ref[:, cols]
        slabs = POOL_GROUP_DIM // LANES
        for c in range(slabs):
            o_ref[g * slabs + c] = y[:, c * LANES:(c + 1) * LANES].astype(o_ref.dtype)


def _pool(x, w_in, w_groups, scale, batch, seq, tm):
    n_s = seq // tm
    return pl.pallas_call(
        functools.partial(_pool_kernel, tm=tm),
        grid=(batch, n_s),
        in_specs=[pl.BlockSpec((tm, D_MODEL), lambda b, i: (b * n_s + i, 0)),
                  _const_spec(w_in.shape),
                  _const_spec(w_groups.shape),
                  _const_spec(scale.shape)],
        out_specs=[pl.BlockSpec((None, MIX_SLABS, tm, LANES), lambda b, i: (b, 0, i, 0)),
                   pl.BlockSpec((tm, MEM_WIDTH), lambda b, i: (b * n_s + i, 0))],
        out_shape=[jax.ShapeDtypeStruct((batch, MIX_SLABS, seq, LANES), BF16),
                   jax.ShapeDtypeStruct((batch * seq, MEM_WIDTH), BF16)],
        scratch_shapes=[pltpu.VMEM((tm + POOL_HALO, MIX_WIDTH), F32)],
        compiler_params=_params(2),
        name="pool_mixer",
    )(x, w_in, w_groups, scale)


def _diff_attn_kernel(lq1_ref, lk1_ref, lq2_ref, lk2_ref, g_ref, qt_ref, k_ref, vt_ref, o_ref,
                      w_ref, sa_ref, sb_ref, m_ref, acc_ref, *, tq, lam_init):
    i = pl.program_id(2)
    tk = tq

    qt = qt_ref[...]
    row = lax.broadcasted_iota(jnp.int32, qt.shape, 0)
    zero = jnp.zeros_like(qt)
    w_ref[:, 0:tq] = jnp.where(row < DIFF_QK_DIM, qt, zero)
    w_ref[:, tq:2 * tq] = jnp.where(row >= DIFF_QK_DIM, qt, zero)
    m_ref[...] = jnp.full(m_ref.shape, MASK_VALUE, F32)
    acc_ref[...] = jnp.zeros(acc_ref.shape, F32)

    def scores(j, s_ref):
        start = pl.multiple_of(j * tk, tk)
        s_ref[...] = jnp.dot(k_ref[pl.ds(start, tk), :], w_ref[...], preferred_element_type=F32)

    def softmax_pv(j, s_ref, masked):
        s = s_ref[...]
        if masked:
            kpos = lax.broadcasted_iota(jnp.int32, s.shape, 0)
            qpos = lax.broadcasted_iota(jnp.int32, s.shape, 1) % tq
            s = jnp.where(kpos <= qpos, s, MASK_VALUE)
        m_old = m_ref[...]
        m_new = jnp.maximum(m_old, jnp.max(s, axis=0, keepdims=True))
        alpha = jnp.exp2(m_old - m_new)
        p = jnp.exp2(s - m_new).astype(BF16)
        acc_ref[...] = alpha * acc_ref[...] + jnp.dot(vt_ref[j], p, preferred_element_type=F32)
        m_ref[...] = m_new

    scores(0, sa_ref)

    def pair(jj):
        j = 2 * jj
        scores(j + 1, sb_ref)
        softmax_pv(j, sa_ref, masked=False)
        scores(j + 2, sa_ref)
        softmax_pv(j + 1, sb_ref, masked=False)

    def pairs_loop(first, n_trips, per_trip):
        def trip(t, carry):
            for u in range(per_trip):
                pair(first + t * per_trip + u)
            return carry
        lax.fori_loop(0, n_trips, trip, 0)

    n_pairs = i // 2
    done = 0
    for per_trip in PAIRS_PER_TRIP:
        n_trips = (n_pairs - done) // per_trip
        pairs_loop(done, n_trips, per_trip)
        done = done + n_trips * per_trip

    @pl.when(i % 2 == 0)
    def _():
        softmax_pv(i, sa_ref, masked=True)

    @pl.when(i % 2 == 1)
    def _():
        scores(i, sb_ref)
        softmax_pv(i - 1, sa_ref, masked=False)
        softmax_pv(i, sb_ref, masked=True)

    lam = (jnp.exp(jnp.sum(lq1_ref[...] * lk1_ref[...], axis=-1, keepdims=True))
           - jnp.exp(jnp.sum(lq2_ref[...] * lk2_ref[...], axis=-1, keepdims=True)) + lam_init)
    norm = acc_ref[DIFF_V_DIM:DIFF_V_DIM + 1, :]
    o1 = acc_ref[0:DIFF_V_DIM, 0:tq] / norm[:, 0:tq]
    o2 = acc_ref[0:DIFF_V_DIM, tq:2 * tq] / norm[:, tq:2 * tq]
    ot = o1 - lam * o2
    ot = ot * lax.rsqrt(jnp.mean(ot * ot, axis=0, keepdims=True) + RMS_EPS)
    o_ref[...] = (ot.T * g_ref[...] * (1.0 - lam_init)).astype(o_ref.dtype)


def _diff_attn(qt, k, vt, lq1, lk1, lq2, lk2, g, batch, seq, tq, lam_init):
    n_q = seq // tq
    vec = lambda n: pl.BlockSpec((1, n), lambda b, h, i: (0, 0))
    return pl.pallas_call(
        functools.partial(_diff_attn_kernel, tq=tq, lam_init=lam_init),
        grid=(batch, DIFF_HEADS, n_q),
        in_specs=[vec(DIFF_QK_DIM), vec(DIFF_QK_DIM), vec(DIFF_QK_DIM), vec(DIFF_QK_DIM), vec(DIFF_V_DIM),
                  pl.BlockSpec((None, None, None, LANES, tq), lambda b, h, i: (b, h, i, 0, 0)),
                  pl.BlockSpec((None, None, seq, LANES), lambda b, h, i: (b, h, 0, 0)),
                  pl.BlockSpec((None, None, n_q, VT_ROWS, tq), lambda b, h, i: (b, h, 0, 0, 0))],
        out_specs=pl.BlockSpec((None, None, tq, LANES), lambda b, h, i: (b, h, i, 0)),
        out_shape=jax.ShapeDtypeStruct((batch, DIFF_HEADS, seq, LANES), BF16),
        scratch_shapes=[pltpu.VMEM((LANES, 2 * tq), BF16),
                        pltpu.VMEM((tq, 2 * tq), F32),
                        pltpu.VMEM((tq, 2 * tq), F32),
                        pltpu.VMEM((1, 2 * tq), F32),
                        pltpu.VMEM((VT_ROWS, 2 * tq), F32)],
        compiler_params=_params(3),
        name="diff_attn",
    )(lq1, lk1, lq2, lk2, g, qt, k, vt)


def _post_kernel(mix_ref, qm_ref, kv_ref, w_ref, x_ref, g_ref, b_ref, o_ref, cat_ref):
    for c in range(MIX_SLABS):
        cat_ref[:, c * LANES:(c + 1) * LANES] = mix_ref[c]
    q = qm_ref[...]
    for hd in range(MEM_HEADS):
        lo = hd * MEM_HEAD_DIM
        qh = q[:, lo:lo + MEM_HEAD_DIM]
        kh = kv_ref[:, lo:lo + MEM_HEAD_DIM]
        vh = kv_ref[:, MEM_WIDTH + lo:MEM_WIDTH + lo + MEM_HEAD_DIM]
        s = lax.dot_general(qh, kh, (((1,), (1,)), ((), ())), preferred_element_type=F32)
        s = s * (MEM_HEAD_DIM ** -0.5)
        p = jnp.exp(s - jnp.max(s, axis=-1, keepdims=True))
        denom = jnp.sum(p, axis=-1, keepdims=True)
        oh = jnp.dot(p.astype(BF16), vh, preferred_element_type=F32) / denom
        cat_ref[:, MIX_WIDTH + lo:MIX_WIDTH + lo + MEM_HEAD_DIM] = oh.astype(BF16)
    y = jnp.dot(cat_ref[...], w_ref[...], preferred_element_type=F32)
    o_ref[...] = _layernorm(DEEPNORM_ALPHA * x_ref[...] + y, g_ref[...], b_ref[...])


def _post(mix, qm, kv, w_out, layer, x, g, b, seq, tm):
    t = x.shape[0]
    n_s = seq // tm
    return pl.pallas_call(
        _post_kernel,
        grid=(t // tm,),
        in_specs=[pl.BlockSpec((None, MIX_SLABS, tm, LANES), lambda i: (i // n_s, 0, i % n_s, 0)),
                  pl.BlockSpec((tm, MEM_WIDTH), lambda i: (i, 0)),
                  pl.BlockSpec((MEM_LEN, 2 * MEM_WIDTH), lambda i: (i // n_s, 0)),
                  pl.BlockSpec((None, D_MODEL, D_MODEL), lambda i: (layer, 0, 0),
                               pipeline_mode=pl.Buffered(1)),
                  pl.BlockSpec((tm, D_MODEL), lambda i: (i, 0)),
                  _const_spec(g.shape),
                  _const_spec(b.shape)],
        out_specs=pl.BlockSpec((tm, D_MODEL), lambda i: (i, 0)),
        out_shape=jax.ShapeDtypeStruct((t, D_MODEL), F32),
        scratch_shapes=[pltpu.VMEM((tm, D_MODEL), BF16)],
        compiler_params=_params(1),
        name="post_mix",
    )(mix, qm, kv, w_out, x, g, b)


def _ffn_kernel(x_ref, wup_hbm, cwb_ref, wdn_hbm, g_ref, b_ref, o_ref,
                xb_ref, carry_ref, wg_buf, wu_buf, wd_buf, sem, *h_refs, tm, tf, layer, tiles_per_seq):
    i = pl.program_id(0)
    n_i = pl.num_programs(0)
    n_f = D_FF // tf
    n_pairs = n_f // 2
    assert n_f % 2 == 1 and n_pairs % 2 == 1
    seq_start = (i % tiles_per_seq) == 0

    def chunk_copies(f, slot):
        col = pl.multiple_of(f * tf, tf)
        return (pltpu.make_async_copy(wup_hbm.at[layer, :, pl.ds(col, tf)], wg_buf.at[slot], sem.at[0, slot]),
                pltpu.make_async_copy(wup_hbm.at[layer, :, pl.ds(D_FF + col, tf)], wu_buf.at[slot],
                                      sem.at[1, slot]),
                pltpu.make_async_copy(wdn_hbm.at[layer, pl.ds(col, tf), :], wd_buf.at[slot], sem.at[2, slot]))

    def start(f, slot):
        for copy in chunk_copies(f, slot):
            copy.start()

    def wait(f, slot):
        for copy in chunk_copies(f, slot):
            copy.wait()

    @pl.when(i == 0)
    def _():
        start(0, 0)
        start(1, 1)
        carry_ref[...] = jnp.zeros(carry_ref.shape, F32)

    x = x_ref[...]
    xb_ref[...] = x.astype(BF16)
    o_ref[...] = DEEPNORM_ALPHA * x

    pieces = [slice(c * FFN_SUB, (c + 1) * FFN_SUB) for c in range(tf // FFN_SUB)]

    def compute(f, slot, h_set):
        wg_ref, wu_ref, wd_ref = wg_buf.at[slot], wu_buf.at[slot], wd_buf.at[slot]

        def project(branch, cols, h_ref):
            w_ref = (wg_ref, wu_ref)[branch]
            h = jnp.dot(xb_ref[...], w_ref[:, cols], preferred_element_type=F32)
            prev = carry_ref[f, branch, :, cols]
            h_ref[0:CONV_HALO, :] = jnp.where(seq_start, jnp.zeros_like(prev), prev)
            h_ref[CONV_HALO:CONV_HALO + tm, :] = h
            carry_ref[f, branch, :, cols] = h[tm - CONV_HALO:tm, :]

        def conv(branch, cols, h_ref):
            tap = lambda r: cwb_ref[f, branch, r:r + 1, cols]
            return (tap(CONV_WIDTH)
                    + tap(2) * h_ref[CONV_HALO:CONV_HALO + tm, :]
                    + tap(1) * h_ref[CONV_HALO - 1:CONV_HALO - 1 + tm, :]
                    + tap(0) * h_ref[CONV_HALO - 2:CONV_HALO - 2 + tm, :])

        for c, cols in enumerate(pieces):
            project(0, cols, h_set[2 * c])
            project(1, cols, h_set[2 * c + 1])
        total = None
        for c, cols in enumerate(pieces):
            gate = conv(0, cols, h_set[2 * c])
            up = conv(1, cols, h_set[2 * c + 1])
            act = gate * jax.nn.sigmoid(gate) * up
            part = jnp.dot(act.astype(BF16), wd_ref[cols, :], preferred_element_type=F32)
            total = part if total is None else total + part
        o_ref[...] += total

    n_h = 2 * len(pieces)

    def pair_trip(g, carry):
        base = 2 * (g % 2)
        other = 2 - base
        f0 = 2 * g
        wait(f0, base)
        wait(f0 + 1, base + 1)
        start(f0 + 2, other)

        @pl.when(g < n_pairs - 1)
        def _():
            start(f0 + 3, other + 1)

        compute(f0, base, h_refs[0:n_h])
        compute(f0 + 1, base + 1, h_refs[n_h:2 * n_h])
        return carry

    lax.fori_loop(0, n_pairs, pair_trip, 0)

    wait(n_f - 1, 2)

    @pl.when(i + 1 < n_i)
    def _():
        start(0, 0)
        start(1, 1)

    compute(n_f - 1, 2, h_refs[0:n_h])
    o_ref[...] = _layernorm(o_ref[...], g_ref[...], b_ref[...])


def _conv_operands(conv_w, conv_b, tf):
    n_l = conv_w.shape[0]
    n_f = D_FF // tf
    cwb = jnp.concatenate([conv_w, conv_b[:, None, :]], axis=1)
    return cwb.reshape(n_l, CONV_WIDTH + 1, 2, n_f, tf).transpose(0, 3, 2, 1, 4)


def _ffn(x, w_up, cwb, w_down, layer, g, b, seq, tm, tf):
    t = x.shape[0]
    n_f = D_FF // tf
    n_slots = 4
    return pl.pallas_call(
        functools.partial(_ffn_kernel, tm=tm, tf=tf, layer=layer, tiles_per_seq=seq // tm),
        grid=(t // tm,),
        in_specs=[pl.BlockSpec((tm, D_MODEL), lambda i: (i, 0)),
                  pl.BlockSpec(memory_space=pl.ANY),
                  pl.BlockSpec((None, n_f, 2, CONV_WIDTH + 1, tf), lambda i: (layer, 0, 0, 0, 0),
                               pipeline_mode=pl.Buffered(1)),
                  pl.BlockSpec(memory_space=pl.ANY),
                  _const_spec(g.shape),
                  _const_spec(b.shape)],
        out_specs=pl.BlockSpec((tm, D_MODEL), lambda i: (i, 0)),
        out_shape=jax.ShapeDtypeStruct((t, D_MODEL), F32),
        scratch_shapes=[pltpu.VMEM((tm, D_MODEL), BF16),
                        pltpu.VMEM((n_f, 2, CONV_HALO, tf), F32),
                        pltpu.VMEM((n_slots, D_MODEL, tf), BF16),
                        pltpu.VMEM((n_slots, D_MODEL, tf), BF16),
                        pltpu.VMEM((n_slots, tf, D_MODEL), BF16),
                        pltpu.SemaphoreType.DMA((3, n_slots))]
                       + [pltpu.VMEM((tm + CONV_HALO, FFN_SUB), F32)] * (4 * (tf // FFN_SUB)),
        compiler_params=_params(1, FFN_VMEM_LIMIT),
        name="conv_ffn",
    )(x, w_up, cwb, w_down, g, b)


def kernel(x, mem, positions, pool_w_in, pool_w, pool_scale, diff_w_in, diff_lambda_q1, diff_lambda_k1,
           diff_lambda_q2, diff_lambda_k2, diff_subln_g, mem_w_kv, w_out, ln1_g, ln1_b, ffn_w_up,
           ffn_conv_w, ffn_conv_b, ffn_w_down, ln2_g, ln2_b):
    batch, seq, _ = x.shape
    t = batch * seq
    row = lambda a: a.reshape(1, -1)

    xs = x.reshape(t, D_MODEL)
    mem2 = mem.reshape(batch * MEM_LEN, D_MODEL)
    pos = positions.reshape(t, 1)
    inv_freq = ROPE_THETA ** (-jnp.arange(0, ROPE_DIM, 2, dtype=F32) / ROPE_DIM)
    invf = jnp.tile(jnp.tile(inv_freq, DIFF_QK_DIM // ROPE_HALF), LANES // DIFF_QK_DIM).reshape(1, LANES)

    w_out_b = w_out.astype(BF16)
    w_up_b = ffn_w_up.astype(BF16)
    w_down_b = ffn_w_down.astype(BF16)
    cwb = _conv_operands(ffn_conv_w, ffn_conv_b, FFN_CHUNK)

    for i in range(DEPTH):
        j = i // 2
        kv = _matmul(mem2, mem_w_kv[i].astype(BF16), BF16, tm=batch * MEM_LEN, tn=MEM_WIDTH)
        if i % 2 == 0:
            mix, qm = _pool(xs, pool_w_in[j].astype(BF16), pool_w[j].astype(BF16), row(pool_scale[j]),
                            batch, seq, tm=ROW_TILE)
        else:
            qt, kk, vt, qm = _inproj_rope(xs, diff_w_in[j].astype(BF16), pos, invf, batch, seq, tm=ATTN_TILE)
            lam_init = 0.8 - 0.6 * math.exp(-0.3 * i)
            mix = _diff_attn(qt, kk, vt, row(diff_lambda_q1[j]),
                             row(diff_lambda_k1[j]), row(diff_lambda_q2[j]), row(diff_lambda_k2[j]),
                             row(diff_subln_g[j]), batch, seq, tq=ATTN_TILE, lam_init=lam_init)
        xs = _post(mix, qm, kv, w_out_b, i, xs, row(ln1_g[i]), row(ln1_b[i]), seq, tm=ROW_TILE)
        xs = _ffn(xs, w_up_b, cwb, w_down_b, i, row(ln2_g[i]), row(ln2_b[i]), seq, tm=ROW_TILE, tf=FFN_CHUNK)
    return xs.reshape(batch, seq, D_MODEL)
```

```python
import functools
import math

import jax
import jax.numpy as jnp
from jax import lax
from jax.experimental import pallas as pl
from jax.experimental.pallas import tpu as pltpu

D_MODEL = 2048
DEPTH = 2
MEM_LEN = 256
MEM_HEADS = 4
MEM_HEAD_DIM = 128
MEM_WIDTH = MEM_HEADS * MEM_HEAD_DIM
MIX_WIDTH = D_MODEL - MEM_WIDTH
POOL_WINDOWS = (2, 4, 8, 16)
POOL_GROUP_DIM = MIX_WIDTH // len(POOL_WINDOWS)
POOL_HALO = 16
DIFF_HEADS = 12
DIFF_QK_DIM = 64
DIFF_V_DIM = 2 * DIFF_QK_DIM
VT_ROWS = DIFF_V_DIM + 16
ROPE_DIM = DIFF_QK_DIM // 4
ROPE_HALF = ROPE_DIM // 2
ROPE_THETA = 500000.0
D_FF = 5632
CONV_WIDTH = 3
CONV_HALO = 8
LN_EPS = 1e-5
RMS_EPS = 1e-6
DEEPNORM_ALPHA = (2.0 * DEPTH) ** 0.25
DIFF_IN_WIDTH = 3 * MIX_WIDTH + MEM_WIDTH
MASK_VALUE = -1e30
QK_SCALE_LOG2E = DIFF_QK_DIM ** -0.5 * math.log2(math.e)

LANES = 128
MXU_WIDTH = 256
MIX_SLABS = MIX_WIDTH // LANES
ROW_TILE = 512
FFN_SUB = MXU_WIDTH
FFN_CHUNK = 512
ATTN_TILE = 512
PAIRS_PER_TRIP = (8, 4, 2, 1)
VMEM_LIMIT = 56 * 1024 * 1024
FFN_VMEM_LIMIT = 60 * 1024 * 1024

BF16 = jnp.bfloat16
F32 = jnp.float32


def _params(n_axes, vmem_limit=VMEM_LIMIT):
    return pltpu.CompilerParams(dimension_semantics=("arbitrary",) * n_axes,
                                vmem_limit_bytes=vmem_limit)


def _const_spec(shape):
    return pl.BlockSpec(shape, lambda *_: (0,) * len(shape), pipeline_mode=pl.Buffered(1))


def _layernorm(z, g, b):
    mu = jnp.mean(z, axis=-1, keepdims=True)
    zc = z - mu
    var = jnp.mean(zc * zc, axis=-1, keepdims=True)
    return zc * lax.rsqrt(var + LN_EPS) * g + b


def _matmul_kernel(x_ref, w_ref, o_ref, xb_ref):
    @pl.when(pl.program_id(1) == 0)
    def _():
        xb_ref[...] = x_ref[...].astype(BF16)

    o_ref[...] = jnp.dot(xb_ref[...], w_ref[...], preferred_element_type=F32).astype(o_ref.dtype)


def _matmul(x, w, out_dtype, tm, tn):
    m, k = x.shape
    n = w.shape[1]
    return pl.pallas_call(
        _matmul_kernel,
        grid=(m // tm, n // tn),
        in_specs=[pl.BlockSpec((tm, k), lambda i, j: (i, 0)),
                  pl.BlockSpec((k, tn), lambda i, j: (0, j))],
        out_specs=pl.BlockSpec((tm, tn), lambda i, j: (i, j)),
        out_shape=jax.ShapeDtypeStruct((m, n), out_dtype),
        scratch_shapes=[pltpu.VMEM((tm, k), BF16)],
        compiler_params=_params(2),
        name="matmul",
    )(x, w)


def _inproj_rope_kernel(x_ref, w_ref, pos_ref, invf_ref, qt_ref, k_ref, vt_ref, qm_ref, xb_ref, tab_ref):
    xb_ref[...] = x_ref[...].astype(BF16)
    ang = pos_ref[...].astype(F32) * invf_ref[...]
    cos = jnp.cos(ang)
    sin = jnp.sin(ang)
    r = lax.broadcasted_iota(jnp.int32, ang.shape, 1) % DIFF_QK_DIM
    tab_ref[0] = jnp.where(r < ROPE_DIM, cos, 1.0)
    tab_ref[1] = jnp.where(r < ROPE_HALF, -sin, 0.0)
    tab_ref[2] = jnp.where((r >= ROPE_HALF) & (r < ROPE_DIM), sin, 0.0)

    def rope(a):
        up = pltpu.roll(a, LANES - ROPE_HALF, 1)
        dn = pltpu.roll(a, ROPE_HALF, 1)
        return a * tab_ref[0] + up * tab_ref[1] + dn * tab_ref[2]

    per = MXU_WIDTH // LANES
    for piece in range(DIFF_IN_WIDTH // MXU_WIDTH):
        acc = jnp.dot(xb_ref[...], w_ref[:, piece * MXU_WIDTH:(piece + 1) * MXU_WIDTH],
                      preferred_element_type=F32)
        for c in range(per):
            a = acc[:, c * LANES:(c + 1) * LANES]
            kind, idx = divmod(piece * per + c, DIFF_HEADS)
            if kind == 0:
                qt_ref[idx] = (rope(a) * QK_SCALE_LOG2E).T.astype(BF16)
            elif kind == 1:
                k_ref[idx] = rope(a).astype(BF16)
            elif kind == 2:
                vt_ref[idx, 0:DIFF_V_DIM, :] = a.T.astype(BF16)
                vt_ref[idx, DIFF_V_DIM:VT_ROWS, :] = jnp.ones((VT_ROWS - DIFF_V_DIM, a.shape[0]), BF16)
            else:
                qm_ref[:, idx * LANES:(idx + 1) * LANES] = a.astype(BF16)


def _inproj_rope(x, w, pos, invf, batch, seq, tm):
    m, k = x.shape
    n_s = seq // tm
    return pl.pallas_call(
        _inproj_rope_kernel,
        grid=(m // tm,),
        in_specs=[pl.BlockSpec((tm, k), lambda i: (i, 0)),
                  _const_spec(w.shape),
                  pl.BlockSpec((tm, 1), lambda i: (i, 0)),
                  _const_spec(invf.shape)],
        out_specs=[pl.BlockSpec((None, DIFF_HEADS, None, LANES, tm), lambda i: (i // n_s, 0, i % n_s, 0, 0)),
                   pl.BlockSpec((None, DIFF_HEADS, tm, LANES), lambda i: (i // n_s, 0, i % n_s, 0)),
                   pl.BlockSpec((None, DIFF_HEADS, None, VT_ROWS, tm), lambda i: (i // n_s, 0, i % n_s, 0, 0)),
                   pl.BlockSpec((tm, MEM_WIDTH), lambda i: (i, 0))],
        out_shape=[jax.ShapeDtypeStruct((batch, DIFF_HEADS, n_s, LANES, tm), BF16),
                   jax.ShapeDtypeStruct((batch, DIFF_HEADS, seq, LANES), BF16),
                   jax.ShapeDtypeStruct((batch, DIFF_HEADS, n_s, VT_ROWS, tm), BF16),
                   jax.ShapeDtypeStruct((m, MEM_WIDTH), BF16)],
        scratch_shapes=[pltpu.VMEM((tm, k), BF16), pltpu.VMEM((3, tm, LANES), F32)],
        compiler_params=_params(1),
        name="inproj_rope",
    )(x, w, pos, invf)


def _pool_kernel(x_ref, win_ref, w_ref, scale_ref, o_ref, qm_ref, ext_ref, *, tm):
    i = pl.program_id(1)

    @pl.when(i == 0)
    def _():
        ext_ref[0:POOL_HALO, :] = jnp.zeros((POOL_HALO, MIX_WIDTH), F32)

    @pl.when(i > 0)
    def _():
        ext_ref[0:POOL_HALO, :] = ext_ref[tm:tm + POOL_HALO, :]

    xb = x_ref[...].astype(BF16)
    ext_ref[POOL_HALO:POOL_HALO + tm, :] = jnp.dot(xb, win_ref[:, 0:MIX_WIDTH], preferred_element_type=F32)
    qm_ref[...] = jnp.dot(xb, win_ref[:, MIX_WIDTH:D_MODEL], preferred_element_type=F32).astype(BF16)

    pos = i * tm + lax.broadcasted_iota(jnp.int32, (tm, 1), 0)
    for g, win in enumerate(POOL_WINDOWS):
        cols = slice(g * POOL_GROUP_DIM, (g + 1) * POOL_GROUP_DIM)
        cur = ext_ref[POOL_HALO:POOL_HALO + tm, cols]
        tot = cur
        for lag in range(1, win):
            tot = tot + ext_ref[POOL_HALO - lag:POOL_HALO - lag + tm, cols]
        cnt = jnp.minimum(pos + 1, win).astype(F32)
        d = tot / cnt - cur
        y = jnp.dot(d.astype(BF16), w_ref[g], preferred_element_type=F32) * scale_ref[:, cols]
        slabs = POOL_GROUP_DIM // LANES
        for c in range(slabs):
            o_ref[g * slabs + c] = y[:, c * LANES:(c + 1) * LANES].astype(o_ref.dtype)


def _pool(x, w_in, w_groups, scale, batch, seq, tm):
    n_s = seq // tm
    return pl.pallas_call(
        functools.partial(_pool_kernel, tm=tm),
        grid=(batch, n_s),
        in_specs=[pl.BlockSpec((tm, D_MODEL), lambda b, i: (b * n_s + i, 0)),
                  _const_spec(w_in.shape),
                  _const_spec(w_groups.shape),
                  _const_spec(scale.shape)],
        out_specs=[pl.BlockSpec((None, MIX_SLABS, tm, LANES), lambda b, i: (b, 0, i, 0)),
                   pl.BlockSpec((tm, MEM_WIDTH), lambda b, i: (b * n_s + i, 0))],
        out_shape=[jax.ShapeDtypeStruct((batch, MIX_SLABS, seq, LANES), BF16),
                   jax.ShapeDtypeStruct((batch * seq, MEM_WIDTH), BF16)],
        scratch_shapes=[pltpu.VMEM((tm + POOL_HALO, MIX_WIDTH), F32)],
        compiler_params=_params(2),
        name="pool_mixer",
    )(x, w_in, w_groups, scale)


def _diff_attn_kernel(lq1_ref, lk1_ref, lq2_ref, lk2_ref, g_ref, qt_ref, k_ref, vt_ref, o_ref,
                      w_ref, sa_ref, sb_ref, m_ref, acc_ref, *, tq, lam_init):
    i = pl.program_id(2)
    tk = tq

    qt = qt_ref[...]
    row = lax.broadcasted_iota(jnp.int32, qt.shape, 0)
    zero = jnp.zeros_like(qt)
    w_ref[:, 0:tq] = jnp.where(row < DIFF_QK_DIM, qt, zero)
    w_ref[:, tq:2 * tq] = jnp.where(row >= DIFF_QK_DIM, qt, zero)
    m_ref[...] = jnp.full(m_ref.shape, MASK_VALUE, F32)
    acc_ref[...] = jnp.zeros(acc_ref.shape, F32)

    def scores(j, s_ref):
        start = pl.multiple_of(j * tk, tk)
        s_ref[...] = jnp.dot(k_ref[pl.ds(start, tk), :], w_ref[...], preferred_element_type=F32)

    def softmax_pv(j, s_ref, masked):
        s = s_ref[...]
        if masked:
            kpos = lax.broadcasted_iota(jnp.int32, s.shape, 0)
            qpos = lax.broadcasted_iota(jnp.int32, s.shape, 1) % tq
            s = jnp.where(kpos <= qpos, s, MASK_VALUE)
        m_old = m_ref[...]
        m_new = jnp.maximum(m_old, jnp.max(s, axis=0, keepdims=True))
        alpha = jnp.exp2(m_old - m_new)
        p = jnp.exp2(s - m_new).astype(BF16)
        acc_ref[...] = alpha * acc_ref[...] + jnp.dot(vt_ref[j], p, preferred_element_type=F32)
        m_ref[...] = m_new

    scores(0, sa_ref)

    def pair(jj):
        j = 2 * jj
        scores(j + 1, sb_ref)
        softmax_pv(j, sa_ref, masked=False)
        scores(j + 2, sa_ref)
        softmax_pv(j + 1, sb_ref, masked=False)

    def pairs_loop(first, n_trips, per_trip):
        def trip(t, carry):
            for u in range(per_trip):
                pair(first + t * per_trip + u)
            return carry
        lax.fori_loop(0, n_trips, trip, 0)

    n_pairs = i // 2
    done = 0
    for per_trip in PAIRS_PER_TRIP:
        n_trips = (n_pairs - done) // per_trip
        pairs_loop(done, n_trips, per_trip)
        done = done + n_trips * per_trip

    @pl.when(i % 2 == 0)
    def _():
        softmax_pv(i, sa_ref, masked=True)

    @pl.when(i % 2 == 1)
    def _():
        scores(i, sb_ref)
        softmax_pv(i - 1, sa_ref, masked=False)
        softmax_pv(i, sb_ref, masked=True)

    lam = (jnp.exp(jnp.sum(lq1_ref[...] * lk1_ref[...], axis=-1, keepdims=True))
           - jnp.exp(jnp.sum(lq2_ref[...] * lk2_ref[...], axis=-1, keepdims=True)) + lam_init)
    norm = acc_ref[DIFF_V_DIM:DIFF_V_DIM + 1, :]
    o1 = acc_ref[0:DIFF_V_DIM, 0:tq] / norm[:, 0:tq]
    o2 = acc_ref[0:DIFF_V_DIM, tq:2 * tq] / norm[:, tq:2 * tq]
    ot = o1 - lam * o2
    ot = ot * lax.rsqrt(jnp.mean(ot * ot, axis=0, keepdims=True) + RMS_EPS)
    o_ref[...] = (ot.T * g_ref[...] * (1.0 - lam_init)).astype(o_ref.dtype)


def _diff_attn(qt, k, vt, lq1, lk1, lq2, lk2, g, batch, seq, tq, lam_init):
    n_q = seq // tq
    vec = lambda n: pl.BlockSpec((1, n), lambda b, h, i: (0, 0))
    return pl.pallas_call(
        functools.partial(_diff_attn_kernel, tq=tq, lam_init=lam_init),
        grid=(batch, DIFF_HEADS, n_q),
        in_specs=[vec(DIFF_QK_DIM), vec(DIFF_QK_DIM), vec(DIFF_QK_DIM), vec(DIFF_QK_DIM), vec(DIFF_V_DIM),
                  pl.BlockSpec((None, None, None, LANES, tq), lambda b, h, i: (b, h, i, 0, 0)),
                  pl.BlockSpec((None, None, seq, LANES), lambda b, h, i: (b, h, 0, 0)),
                  pl.BlockSpec((None, None, n_q, VT_ROWS, tq), lambda b, h, i: (b, h, 0, 0, 0))],
        out_specs=pl.BlockSpec((None, None, tq, LANES), lambda b, h, i: (b, h, i, 0)),
        out_shape=jax.ShapeDtypeStruct((batch, DIFF_HEADS, seq, LANES), BF16),
        scratch_shapes=[pltpu.VMEM((LANES, 2 * tq), BF16),
                        pltpu.VMEM((tq, 2 * tq), F32),
                        pltpu.VMEM((tq, 2 * tq), F32),
                        pltpu.VMEM((1, 2 * tq), F32),
                        pltpu.VMEM((VT_ROWS, 2 * tq), F32)],
        compiler_params=_params(3),
        name="diff_attn",
    )(lq1, lk1, lq2, lk2, g, qt, k, vt)


def _post_kernel(mix_ref, qm_ref, kv_ref, w_ref, x_ref, g_ref, b_ref, o_ref, cat_ref):
    for c in range(MIX_SLABS):
        cat_ref[:, c * LANES:(c + 1) * LANES] = mix_ref[c]
    q = qm_ref[...]
    for hd in range(MEM_HEADS):
        lo = hd * MEM_HEAD_DIM
        qh = q[:, lo:lo + MEM_HEAD_DIM]
        kh = kv_ref[:, lo:lo + MEM_HEAD_DIM]
        vh = kv_ref[:, MEM_WIDTH + lo:MEM_WIDTH + lo + MEM_HEAD_DIM]
        s = lax.dot_general(qh, kh, (((1,), (1,)), ((), ())), preferred_element_type=F32)
        s = s * (MEM_HEAD_DIM ** -0.5)
        p = jnp.exp(s - jnp.max(s, axis=-1, keepdims=True))
        denom = jnp.sum(p, axis=-1, keepdims=True)
        oh = jnp.dot(p.astype(BF16), vh, preferred_element_type=F32) / denom
        cat_ref[:, MIX_WIDTH + lo:MIX_WIDTH + lo + MEM_HEAD_DIM] = oh.astype(BF16)
    y = jnp.dot(cat_ref[...], w_ref[...], preferred_element_type=F32)
    o_ref[...] = _layernorm(DEEPNORM_ALPHA * x_ref[...] + y, g_ref[...], b_ref[...])


def _post(mix, qm, kv, w_out, layer, x, g, b, seq, tm):
    t = x.shape[0]
    n_s = seq // tm
    return pl.pallas_call(
        _post_kernel,
        grid=(t // tm,),
        in_specs=[pl.BlockSpec((None, MIX_SLABS, tm, LANES), lambda i: (i // n_s, 0, i % n_s, 0)),
                  pl.BlockSpec((tm, MEM_WIDTH), lambda i: (i, 0)),
                  pl.BlockSpec((MEM_LEN, 2 * MEM_WIDTH), lambda i: (i // n_s, 0)),
                  pl.BlockSpec((None, D_MODEL, D_MODEL), lambda i: (layer, 0, 0),
                               pipeline_mode=pl.Buffered(1)),
                  pl.BlockSpec((tm, D_MODEL), lambda i: (i, 0)),
                  _const_spec(g.shape),
                  _const_spec(b.shape)],
        out_specs=pl.BlockSpec((tm, D_MODEL), lambda i: (i, 0)),
        out_shape=jax.ShapeDtypeStruct((t, D_MODEL), F32),
        scratch_shapes=[pltpu.VMEM((tm, D_MODEL), BF16)],
        compiler_params=_params(1),
        name="post_mix",
    )(mix, qm, kv, w_out, x, g, b)


def _ffn_kernel(x_ref, wup_hbm, cwb_ref, wdn_hbm, g_ref, b_ref, o_ref,
                xb_ref, carry_ref, wg_buf, wu_buf, wd_buf, sem, *h_refs, tm, tf, layer, tiles_per_seq):
    i = pl.program_id(0)
    n_i = pl.num_programs(0)
    n_f = D_FF // tf
    n_pairs = n_f // 2
    assert n_f % 2 == 1 and n_pairs % 2 == 1
    seq_start = (i % tiles_per_seq) == 0

    def chunk_copies(f, slot):
        col = pl.multiple_of(f * tf, tf)
        return (pltpu.make_async_copy(wup_hbm.at[layer, :, pl.ds(col, tf)], wg_buf.at[slot], sem.at[0, slot]),
                pltpu.make_async_copy(wup_hbm.at[layer, :, pl.ds(D_FF + col, tf)], wu_buf.at[slot],
                                      sem.at[1, slot]),
                pltpu.make_async_copy(wdn_hbm.at[layer, pl.ds(col, tf), :], wd_buf.at[slot], sem.at[2, slot]))

    def start(f, slot):
        for copy in chunk_copies(f, slot):
            copy.start()

    def wait(f, slot):
        for copy in chunk_copies(f, slot):
            copy.wait()

    @pl.when(i == 0)
    def _():
        start(0, 0)
        start(1, 1)
        carry_ref[...] = jnp.zeros(carry_ref.shape, F32)

    x = x_ref[...]
    xb_ref[...] = x.astype(BF16)
    o_ref[...] = DEEPNORM_ALPHA * x

    pieces = [slice(c * FFN_SUB, (c + 1) * FFN_SUB) for c in range(tf // FFN_SUB)]

    n_h = 2 * len(pieces)

    def compute(chunks):
        def project(f, slot, branch, cols, h_ref):
            w_ref = (wg_buf, wu_buf)[branch].at[slot]
            h = jnp.dot(xb_ref[...], w_ref[:, cols], preferred_element_type=F32)
            prev = carry_ref[f, branch, :, cols]
            h_ref[0:CONV_HALO, :] = jnp.where(seq_start, jnp.zeros_like(prev), prev)
            h_ref[CONV_HALO:CONV_HALO + tm, :] = h
            carry_ref[f, branch, :, cols] = h[tm - CONV_HALO:tm, :]

        def conv(f, branch, cols, h_ref):
            tap = lambda r: cwb_ref[f, branch, r:r + 1, cols]
            return (tap(CONV_WIDTH)
                    + tap(2) * h_ref[CONV_HALO:CONV_HALO + tm, :]
                    + tap(1) * h_ref[CONV_HALO - 1:CONV_HALO - 1 + tm, :]
                    + tap(0) * h_ref[CONV_HALO - 2:CONV_HALO - 2 + tm, :])

        for n, (f, slot) in enumerate(chunks):
            for c, cols in enumerate(pieces):
                project(f, slot, 0, cols, h_refs[n * n_h + 2 * c])
                project(f, slot, 1, cols, h_refs[n * n_h + 2 * c + 1])
        total = None
        for n, (f, slot) in enumerate(chunks):
            for c, cols in enumerate(pieces):
                gate = conv(f, 0, cols, h_refs[n * n_h + 2 * c])
                up = conv(f, 1, cols, h_refs[n * n_h + 2 * c + 1])
                act = gate * jax.nn.sigmoid(gate) * up
                part = jnp.dot(act.astype(BF16), wd_buf.at[slot][cols, :], preferred_element_type=F32)
                total = part if total is None else total + part
        o_ref[...] += total

    def pair_trip(g, carry):
        base = 2 * (g % 2)
        other = 2 - base
        f0 = 2 * g
        wait(f0, base)
        wait(f0 + 1, base + 1)
        start(f0 + 2, other)

        @pl.when(g < n_pairs - 1)
        def _():
            start(f0 + 3, other + 1)

        compute([(f0, base), (f0 + 1, base + 1)])
        return carry

    lax.fori_loop(0, n_pairs, pair_trip, 0)

    wait(n_f - 1, 2)

    @pl.when(i + 1 < n_i)
    def _():
        start(0, 0)
        start(1, 1)

    compute([(n_f - 1, 2)])
    o_ref[...] = _layernorm(o_ref[...], g_ref[...], b_ref[...])


def _conv_operands(conv_w, conv_b, tf):
    n_l = conv_w.shape[0]
    n_f = D_FF // tf
    cwb = jnp.concatenate([conv_w, conv_b[:, None, :]], axis=1)
    return cwb.reshape(n_l, CONV_WIDTH + 1, 2, n_f, tf).transpose(0, 3, 2, 1, 4)


def _ffn(x, w_up, cwb, w_down, layer, g, b, seq, tm, tf):
    t = x.shape[0]
    n_f = D_FF // tf
    n_slots = 4
    return pl.pallas_call(
        functools.partial(_ffn_kernel, tm=tm, tf=tf, layer=layer, tiles_per_seq=seq // tm),
        grid=(t // tm,),
        in_specs=[pl.BlockSpec((tm, D_MODEL), lambda i: (i, 0)),
                  pl.BlockSpec(memory_space=pl.ANY),
                  pl.BlockSpec((None, n_f, 2, CONV_WIDTH + 1, tf), lambda i: (layer, 0, 0, 0, 0),
                               pipeline_mode=pl.Buffered(1)),
                  pl.BlockSpec(memory_space=pl.ANY),
                  _const_spec(g.shape),
                  _const_spec(b.shape)],
        out_specs=pl.BlockSpec((tm, D_MODEL), lambda i: (i, 0)),
        out_shape=jax.ShapeDtypeStruct((t, D_MODEL), F32),
        scratch_shapes=[pltpu.VMEM((tm, D_MODEL), BF16),
                        pltpu.VMEM((n_f, 2, CONV_HALO, tf), F32),
                        pltpu.VMEM((n_slots, D_MODEL, tf), BF16),
                        pltpu.VMEM((n_slots, D_MODEL, tf), BF16),
                        pltpu.VMEM((n_slots, tf, D_MODEL), BF16),
                        pltpu.SemaphoreType.DMA((3, n_slots))]
                       + [pltpu.VMEM((tm + CONV_HALO, FFN_SUB), F32)] * (4 * (tf // FFN_SUB)),
        compiler_params=_params(1, FFN_VMEM_LIMIT),
        name="conv_ffn",
    )(x, w_up, cwb, w_down, g, b)


def kernel(x, mem, positions, pool_w_in, pool_w, pool_scale, diff_w_in, diff_lambda_q1, diff_lambda_k1,
           diff_lambda_q2, diff_lambda_k2, diff_subln_g, mem_w_kv, w_out, ln1_g, ln1_b, ffn_w_up,
           ffn_conv_w, ffn_conv_b, ffn_w_down, ln2_g, ln2_b):
    batch, seq, _ = x.shape
    t = batch * seq
    row = lambda a: a.reshape(1, -1)

    xs = x.reshape(t, D_MODEL)
    mem2 = mem.reshape(batch * MEM_LEN, D_MODEL)
    pos = positions.reshape(t, 1)
    inv_freq = ROPE_THETA ** (-jnp.arange(0, ROPE_DIM, 2, dtype=F32) / ROPE_DIM)
    invf = jnp.tile(jnp.tile(inv_freq, DIFF_QK_DIM // ROPE_HALF), LANES // DIFF_QK_DIM).reshape(1, LANES)

    w_out_b = w_out.astype(BF16)
    w_up_b = ffn_w_up.astype(BF16)
    w_down_b = ffn_w_down.astype(BF16)
    cwb = _conv_operands(ffn_conv_w, ffn_conv_b, FFN_CHUNK)

    for i in range(DEPTH):
        j = i // 2
        kv = _matmul(mem2, mem_w_kv[i].astype(BF16), BF16, tm=batch * MEM_LEN, tn=MEM_WIDTH)
        if i % 2 == 0:
            mix, qm = _pool(xs, pool_w_in[j].astype(BF16), pool_w[j].astype(BF16), row(pool_scale[j]),
                            batch, seq, tm=ROW_TILE)
        else:
            qt, kk, vt, qm = _inproj_rope(xs, diff_w_in[j].astype(BF16), pos, invf, batch, seq, tm=ATTN_TILE)
            lam_init = 0.8 - 0.6 * math.exp(-0.3 * i)
            mix = _diff_attn(qt, kk, vt, row(diff_lambda_q1[j]),
                             row(diff_lambda_k1[j]), row(diff_lambda_q2[j]), row(diff_lambda_k2[j]),
                             row(diff_subln_g[j]), batch, seq, tq=ATTN_TILE, lam_init=lam_init)
        xs = _post(mix, qm, kv, w_out_b, i, xs, row(ln1_g[i]), row(ln1_b[i]), seq, tm=ROW_TILE)
        xs = _ffn(xs, w_up_b, cwb, w_down_b, i, row(ln2_g[i]), row(ln2_b[i]), seq, tm=ROW_TILE, tf=FFN_CHUNK)
    return xs.reshape(batch, seq, D_MODEL)
```
